```python
import jax, jax.numpy as jnp
from jax import lax
import numpy as np

D_MODEL = 2048
BATCH = 4
SEQ = 4096
DEPTH = 1

EPS = 1e-6
CONV_A_WIDTH = D_MODEL
CONV_A_K = 3
SSM_EXPAND = 2
SSM_D_INNER = SSM_EXPAND * D_MODEL
SSM_HEAD_DIM = 64
SSM_HEADS = SSM_D_INNER // SSM_HEAD_DIM
SSM_GROUPS = 8
SSM_STATE = 128
SSM_CONV_K = 4
SSM_CONV_DIM = SSM_D_INNER + 2 * SSM_GROUPS * SSM_STATE
SSM_CHUNK = 128
SSM_NORM_GROUP = SSM_D_INNER // SSM_GROUPS
D_FF = -(-(8 * D_MODEL) // (3 * 256)) * 256

OFF_GATE_A = 0
OFF_GATE_B = D_MODEL
OFF_A = 2 * D_MODEL
OFF_Z = OFF_A + 3 * CONV_A_WIDTH
OFF_XBC = OFF_Z + SSM_D_INNER
OFF_DT = OFF_XBC + SSM_CONV_DIM
IN_COLS = OFF_DT + SSM_HEADS

kernel_name = "hybrid_shortconv_ssd_gated_block"


def rms_norm(x, g):
    xf = x.astype(jnp.float32)
    y = xf * lax.rsqrt(jnp.mean(xf * xf, axis=-1, keepdims=True) + EPS)
    return (y * g.astype(jnp.float32)).astype(x.dtype)


def causal_depthwise_conv(u, w):
    k = w.shape[0]
    t = u.shape[1]
    up = jnp.pad(u, ((0, 0), (k - 1, 0), (0, 0)))
    out = up[:, 0:t] * w[0]
    for j in range(1, k):
        out = out + up[:, j:j + t] * w[j]
    return out


def short_gated_conv_mixer(u_a, conv_w, w_out):
    b_gate, c_gate, h = jnp.split(u_a, 3, axis=-1)
    y = b_gate * causal_depthwise_conv(c_gate * h, conv_w)
    return y @ w_out


def ssd_chunked(xs, dt, a, b_ssm, c_ssm):
    bsz, t, h, p = xs.shape
    g, n = b_ssm.shape[2], b_ssm.shape[3]
    r = h // g
    q = SSM_CHUNK
    nc = t // q
    x_c = xs.reshape(bsz, nc, q, g, r, p)
    dt_c = dt.reshape(bsz, nc, q, g, r)
    b_c = b_ssm.reshape(bsz, nc, q, g, n)
    c_c = c_ssm.reshape(bsz, nc, q, g, n)
    xdt = x_c * dt_c[..., None]
    log_a = (dt_c * a.reshape(g, r)).transpose(0, 1, 3, 4, 2)
    cs = jnp.cumsum(log_a, axis=-1)
    causal = jnp.tril(jnp.ones((q, q), dtype=bool))
    seg = jnp.where(causal, cs[..., :, None] - cs[..., None, :], -jnp.inf)
    decay_in = jnp.exp(seg)
    cb = jnp.einsum('bclgn,bcsgn->bcgls', c_c, b_c)
    y_diag = jnp.einsum('bcgrls,bcsgrp->bclgrp', cb[:, :, :, None] * decay_in, xdt)
    decay_to_end = jnp.exp(cs[..., -1:] - cs)
    chunk_states = jnp.einsum('bclgn,bcgrl,bclgrp->bcgrpn', b_c, decay_to_end, xdt)
    chunk_decay = jnp.exp(cs[..., -1])

    def step(state, inp):
        s_new, dec = inp
        return dec[..., None, None] * state + s_new, state

    states_t = jnp.moveaxis(chunk_states, 1, 0)
    decay_t = jnp.moveaxis(chunk_decay, 1, 0)
    _, prev = lax.scan(step, jnp.zeros_like(states_t[0]), (states_t, decay_t))
    prev_states = jnp.moveaxis(prev, 0, 1)
    y_off = jnp.einsum('bclgn,bcgrpn,bcgrl->bclgrp', c_c, prev_states, jnp.exp(cs))
    return (y_diag + y_off).reshape(bsz, t, h, p)


def mamba2_mixer(z, xbc, dt_raw, conv_w, conv_b, dt_bias, a_log, d_skip, norm_g, w_out):
    bsz, t, _ = z.shape
    xbc = jax.nn.silu(causal_depthwise_conv(xbc, conv_w) + conv_b)
    xs = xbc[..., :SSM_D_INNER].reshape(bsz, t, SSM_HEADS, SSM_HEAD_DIM)
    b_ssm = xbc[..., SSM_D_INNER:SSM_D_INNER + SSM_GROUPS * SSM_STATE].reshape(bsz, t, SSM_GROUPS, SSM_STATE)
    c_ssm = xbc[..., SSM_D_INNER + SSM_GROUPS * SSM_STATE:].reshape(bsz, t, SSM_GROUPS, SSM_STATE)
    dt = jax.nn.softplus(dt_raw.astype(jnp.float32) + dt_bias.astype(jnp.float32))
    a = -jnp.exp(a_log.astype(jnp.float32))
    y = ssd_chunked(xs, dt, a, b_ssm, c_ssm)
    y = y + d_skip.astype(jnp.float32)[:, None] * xs
    yz = (y.reshape(bsz, t, SSM_D_INNER) * jax.nn.silu(z.astype(jnp.float32)))
    yz = yz.reshape(bsz, t, SSM_GROUPS, SSM_NORM_GROUP)
    yz = yz * lax.rsqrt(jnp.mean(yz * yz, axis=-1, keepdims=True) + EPS)
    yz = yz.reshape(bsz, t, SSM_D_INNER) * norm_g.astype(jnp.float32)
    return yz.astype(z.dtype) @ w_out


def swiglu(u, w_gate, w_up, w_down):
    return (jax.nn.silu(u @ w_gate) * (u @ w_up)) @ w_down


def setup_inputs(seed: int = 0) -> dict:
    key = jax.random.key(seed)
    ks = jax.random.split(key, 20)
    L, D = DEPTH, D_MODEL
    f32 = jnp.float32

    def nrm(k, shape, fan_in):
        return jax.random.normal(k, shape, f32) * (fan_in ** -0.5)

    def gain(k, shape):
        return 1.0 + 0.02 * jax.random.normal(k, shape, f32)

    dt0 = jnp.exp(jax.random.uniform(ks[7], (L, SSM_HEADS), f32) * (np.log(0.1) - np.log(0.001)) + np.log(0.001))
    dt_bias = dt0 + jnp.log(-jnp.expm1(-dt0))
    a_log = jnp.log(jax.random.uniform(ks[8], (L, SSM_HEADS), f32, 1.0, 16.0))
    return {
        "x": jax.random.normal(ks[0], (BATCH, SEQ, D), f32),
        "norm_mix_g": gain(ks[1], (L, D)),
        "w_in": nrm(ks[2], (L, D, IN_COLS), D),
        "conv_a_w": nrm(ks[3], (L, CONV_A_K, CONV_A_WIDTH), CONV_A_K),
        "w_out_a": nrm(ks[4], (L, CONV_A_WIDTH, D), CONV_A_WIDTH),
        "ssm_conv_w": nrm(ks[5], (L, SSM_CONV_K, SSM_CONV_DIM), SSM_CONV_K),
        "ssm_conv_b": 0.02 * jax.random.normal(ks[6], (L, SSM_CONV_DIM), f32),
        "dt_bias": dt_bias,
        "a_log": a_log,
        "d_skip": gain(ks[9], (L, SSM_HEADS)),
        "ssm_norm_g": gain(ks[10], (L, SSM_D_INNER)),
        "w_out_ssm": nrm(ks[11], (L, SSM_D_INNER, D), SSM_D_INNER),
        "w_o": nrm(ks[12], (L, D, D), D),
        "norm_ffn_g": gain(ks[13], (L, D)),
        "w_ffn_gate": nrm(ks[14], (L, D, D_FF), D),
        "w_ffn_up": nrm(ks[15], (L, D, D_FF), D),
        "w_ffn_down": nrm(ks[16], (L, D_FF, D), D_FF),
        "norm_final_g": gain(ks[17], (D,)),
    }


def reference(x, norm_mix_g, w_in, conv_a_w, w_out_a, ssm_conv_w, ssm_conv_b, dt_bias, a_log,
              d_skip, ssm_norm_g, w_out_ssm, w_o, norm_ffn_g, w_ffn_gate, w_ffn_up, w_ffn_down,
              norm_final_g):
    h = x
    for i in range(DEPTH):
        u = rms_norm(h, norm_mix_g[i])
        proj = u @ w_in[i]
        gate_a = jax.nn.sigmoid(proj[..., OFF_GATE_A:OFF_GATE_B])
        gate_b = jax.nn.sigmoid(proj[..., OFF_GATE_B:OFF_A])
        y_a = short_gated_conv_mixer(proj[..., OFF_A:OFF_Z], conv_a_w[i], w_out_a[i])
        y_b = mamba2_mixer(proj[..., OFF_Z:OFF_XBC], proj[..., OFF_XBC:OFF_DT], proj[..., OFF_DT:IN_COLS],
                           ssm_conv_w[i], ssm_conv_b[i], dt_bias[i], a_log[i], d_skip[i],
                           ssm_norm_g[i], w_out_ssm[i])
        h = h + (gate_a * y_a + gate_b * y_b) @ w_o[i]
        h = h + swiglu(rms_norm(h, norm_ffn_g[i]), w_ffn_gate[i], w_ffn_up[i], w_ffn_down[i])
    return rms_norm(h, norm_final_g)
```

```python
import functools

import jax
import jax.numpy as jnp
from jax import lax
from jax.experimental import pallas as pl
from jax.experimental.pallas import tpu as pltpu

EPS = 1e-6
SSM_GROUPS = 8
SSM_STATE = 128
SSM_CHUNK = 128

LANES = 128
BF16_SUBLANES = 16
F32_SUBLANES = 8
V7X_VMEM_LIMIT_BYTES = 60000 * 1024

F32 = jnp.float32
BF16 = jnp.bfloat16


def _params(semantics):
    return pltpu.CompilerParams(dimension_semantics=semantics,
                                vmem_limit_bytes=V7X_VMEM_LIMIT_BYTES)


def _resident(shape):
    return pl.BlockSpec(shape, lambda *_: (0,) * len(shape), pipeline_mode=pl.Buffered(1))


def _rmsnorm(x, g):
    return x * lax.rsqrt(jnp.mean(x * x, axis=-1, keepdims=True) + EPS) * g


def _dot(a, b):
    return jnp.dot(a, b, preferred_element_type=F32)


def _in_proj_kernel(x_ref, g_ref, w_ref, wdt_ref, proj_ref, dt_ref, u_ref):
    @pl.when(pl.program_id(1) == 0)
    def _():
        u = _rmsnorm(x_ref[...], g_ref[...]).astype(BF16)
        u_ref[...] = u
        dt_ref[...] = _dot(u, wdt_ref[...])

    proj_ref[...] = _dot(u_ref[...], w_ref[...]).astype(BF16)


def _in_proj(x2, g, w_main, w_dt, *, tm, tn):
    m, d = x2.shape
    n = w_main.shape[1]
    return pl.pallas_call(
        _in_proj_kernel,
        grid=(m // tm, n // tn),
        in_specs=[
            pl.BlockSpec((tm, d), lambda i, j: (i, 0)),
            pl.BlockSpec((1, d), lambda i, j: (0, 0)),
            pl.BlockSpec((d, tn), lambda i, j: (0, j)),
            pl.BlockSpec((d, LANES), lambda i, j: (0, 0)),
        ],
        out_specs=[
            pl.BlockSpec((tm, tn), lambda i, j: (i, j)),
            pl.BlockSpec((tm, LANES), lambda i, j: (i, 0)),
        ],
        out_shape=[jax.ShapeDtypeStruct((m, n), BF16),
                   jax.ShapeDtypeStruct((m, LANES), F32)],
        scratch_shapes=[pltpu.VMEM((tm, d), BF16)],
        compiler_params=_params(("arbitrary", "arbitrary")),
        name="in_proj",
    )(x2, g, w_main, w_dt)


def _mixer_a_kernel(ga_ref, b_ref, c_ref, h_ref, ch_ref, hh_ref, cw_ref, wo_ref, out_ref,
                    buf_ref, *, tiles_per_seq):
    tm = c_ref.shape[0]
    halo = ch_ref.shape[0]
    taps = cw_ref.shape[0]
    v = c_ref[...].astype(F32) * h_ref[...].astype(F32)
    vh = ch_ref[...].astype(F32) * hh_ref[...].astype(F32)
    seq_start = pl.program_id(0) % tiles_per_seq == 0
    buf_ref[0:halo, :] = jnp.where(seq_start, 0.0, vh)
    buf_ref[halo:, :] = v
    acc = cw_ref[taps - 1:taps, :] * v
    for j in range(taps - 1):
        lo = halo - (taps - 1 - j)
        acc = acc + cw_ref[j:j + 1, :] * buf_ref[lo:lo + tm, :]
    y = (b_ref[...].astype(F32) * acc).astype(BF16)
    ya = _dot(y, wo_ref[...])
    out_ref[...] = (jax.nn.sigmoid(ga_ref[...].astype(F32)) * ya).astype(BF16)


def _mixer_a(proj, conv_w, w_out, *, seq, tm, col_gate, col_b):
    m = proj.shape[0]
    c = conv_w.shape[1]
    d = w_out.shape[1]
    halo = BF16_SUBLANES
    hb = tm // halo

    def col(k):
        return pl.BlockSpec((tm, c), lambda i: (i, k))

    def halo_col(k):
        return pl.BlockSpec((halo, c), lambda i: (jnp.maximum(i * hb - 1, 0), k))

    return pl.pallas_call(
        functools.partial(_mixer_a_kernel, tiles_per_seq=seq // tm),
        grid=(m // tm,),
        in_specs=[col(col_gate), col(col_b), col(col_b + 1), col(col_b + 2),
                  halo_col(col_b + 1), halo_col(col_b + 2),
                  _resident(conv_w.shape), _resident(w_out.shape)],
        out_specs=pl.BlockSpec((tm, d), lambda i: (i, 0)),
        out_shape=jax.ShapeDtypeStruct((m, d), BF16),
        scratch_shapes=[pltpu.VMEM((halo + tm, c), F32)],
        compiler_params=_params(("arbitrary",)),
        name="mixer_a",
    )(proj, proj, proj, proj, proj, proj, conv_w, w_out)


def _ssd_kernel(z0_ref, z1_ref, x0_ref, x1_ref, b_ref, c_ref, dt_ref,
                cw_ref, cb_ref, dtb_ref, alog_ref, dsk_ref, ng_ref, e_ref,
                out_ref,
                cbuf_ref, xbc_ref, st_ref, dte_ref, dee_ref, ecs_ref, cd_ref, csg_ref, cst_ref,
                y_ref):
    q = SSM_CHUNK
    groups = SSM_GROUPS
    taps = cw_ref.shape[1]
    gw = y_ref.shape[2]
    hg = csg_ref.shape[2]
    hp = gw // hg
    n_heads = cst_ref.shape[0]
    slabs_x = groups * gw // LANES
    slabs_g = gw // LANES
    half = x0_ref.shape[1]
    tail = F32_SUBLANES

    @pl.when(pl.program_id(1) == 0)
    def _():
        cbuf_ref[:, 0:tail, :] = jnp.zeros((cbuf_ref.shape[0], tail, LANES), F32)
        st_ref[...] = jnp.zeros(st_ref.shape, F32)

    for s in range(slabs_x):
        src = x0_ref if s * LANES < half else x1_ref
        off = (s * LANES) % half
        cbuf_ref[s, tail:, :] = src[:, off:off + LANES].astype(F32)
    for s in range(groups):
        cbuf_ref[slabs_x + s, tail:, :] = b_ref[:, s * LANES:(s + 1) * LANES].astype(F32)
        cbuf_ref[slabs_x + groups + s, tail:, :] = c_ref[:, s * LANES:(s + 1) * LANES].astype(F32)

    def conv_slab(s, carry):
        w = cw_ref[s]
        acc = cb_ref[s] + w[taps - 1:taps, :] * cbuf_ref[s, tail:, :]
        for j in range(taps - 1):
            lo = tail - (taps - 1 - j)
            acc = acc + w[j:j + 1, :] * cbuf_ref[s, lo:lo + q, :]
        xbc_ref[s] = acc * jax.nn.sigmoid(acc)
        cbuf_ref[s, 0:tail, :] = cbuf_ref[s, q:q + tail, :]
        return carry

    lax.fori_loop(0, cbuf_ref.shape[0], conv_slab, 0)

    dt = jax.nn.softplus(dt_ref[...] + dtb_ref[...])
    log_a = dt * -jnp.exp(alog_ref[...])
    row = lax.broadcasted_iota(jnp.int32, (q, q), 0)
    col = lax.broadcasted_iota(jnp.int32, (q, q), 1)
    causal = col <= row
    cs = jnp.dot(causal.astype(F32), log_a, preferred_element_type=F32,
                 precision=lax.Precision.HIGHEST)
    total = cs[q - 1:q, :]
    e_mat = e_ref[...]

    def expand(v):
        return _dot(v.astype(BF16), e_mat)

    def expand_exact(v):
        hi = v.astype(BF16)
        r1 = v - hi.astype(F32)
        mid = r1.astype(BF16)
        lo = (r1 - mid.astype(F32)).astype(BF16)
        return _dot(hi, e_mat) + _dot(mid, e_mat) + _dot(lo, e_mat)

    dt_e = expand(dt)
    de_e = expand(jnp.exp(total - cs))
    ec_e = expand(jnp.exp(cs))
    cd_e = expand_exact(jnp.broadcast_to(jnp.exp(total), (tail, LANES)))
    cst_ref[...] = cs.T[0:n_heads, :]
    for g in range(groups):
        sl = slice(g * gw, (g + 1) * gw)
        dte_ref[g] = dt_e[:, sl]
        dee_ref[g] = de_e[:, sl]
        ecs_ref[g] = ec_e[:, sl]
        cd_ref[g] = cd_e[:, sl]
        csg_ref[g] = cs[:, g * hg:(g + 1) * hg]

    pair_w = 2 * hp
    left = lax.broadcasted_iota(jnp.int32, (q, pair_w), 1) < hp

    def group_body(g, carry):
        bg = xbc_ref[slabs_x + g].astype(BF16)
        cg = xbc_ref[slabs_x + groups + g].astype(BF16)
        cbm = lax.dot_general(cg, bg, (((1,), (1,)), ((), ())), preferred_element_type=F32)
        xs = jnp.concatenate([xbc_ref[g * slabs_g + k] for k in range(slabs_g)], axis=1)
        xdt = xs * dte_ref[g]
        st = st_ref[g]
        y_off = _dot(cg, st.astype(BF16)) * ecs_ref[g]
        xw = (xdt * dee_ref[g]).astype(BF16)
        upd = lax.dot_general(bg, xw, (((0,), (0,)), ((), ())), preferred_element_type=F32)
        st_ref[g] = st * cd_ref[g][0:1, :] + upd
        csg = csg_ref[g]
        cstg = cst_ref[pl.ds(pl.multiple_of(g * hg, hg), hg), :]
        base = y_off + dsk_ref[g] * xs
        for p in range(hg // 2):
            ms = []
            for k in (2 * p, 2 * p + 1):
                seg = csg[:, k:k + 1] - cstg[k:k + 1, :]
                decay = jnp.exp(jnp.where(causal, seg, -jnp.inf))
                ms.append((cbm * decay).astype(BF16))
            x2 = xdt[:, p * pair_w:(p + 1) * pair_w]
            rhs = jnp.concatenate([jnp.where(left, x2, 0.0).astype(BF16),
                                   jnp.where(left, 0.0, x2).astype(BF16)], axis=0)
            yd = _dot(jnp.concatenate(ms, axis=1), rhs)
            y_ref[g, :, p * pair_w:(p + 1) * pair_w] = yd + base[:, p * pair_w:(p + 1) * pair_w]
        return carry

    lax.fori_loop(0, groups, group_body, 0)

    for g in range(groups):
        src = z0_ref if g * gw < half else z1_ref
        off = (g * gw) % half
        z = src[:, off:off + gw].astype(F32)
        yz = y_ref[g] * (z * jax.nn.sigmoid(z))
        yz = yz * lax.rsqrt(jnp.mean(yz * yz, axis=-1, keepdims=True) + EPS)
        out_ref[:, g * gw:(g + 1) * gw] = (yz * ng_ref[:, g * gw:(g + 1) * gw]).astype(BF16)


def _ssd(proj, dt_raw, conv_w, conv_b, dt_bias, a_log, d_skip_e, norm_g, e_mat, *,
         batch, seq, heads, col_z, col_x, col_b):
    m = proj.shape[0]
    q = SSM_CHUNK
    nc = seq // q
    d_inner = norm_g.shape[1]
    half = d_inner // 2
    gn = SSM_GROUPS * SSM_STATE
    gw = d_inner // SSM_GROUPS
    slabs = conv_w.shape[0]

    def rows(width, k):
        return pl.BlockSpec((q, width), lambda b, c: (b * nc + c, k))

    def whole(a):
        return pl.BlockSpec(a.shape, lambda b, c: (0,) * a.ndim)

    per_group = pltpu.VMEM((SSM_GROUPS, q, gw), F32)
    return pl.pallas_call(
        _ssd_kernel,
        grid=(batch, nc),
        in_specs=[rows(half, col_z), rows(half, col_z + 1), rows(half, col_x), rows(half, col_x + 1),
                  rows(gn, col_b), rows(gn, col_b + 1), rows(LANES, 0),
                  whole(conv_w), whole(conv_b), whole(dt_bias), whole(a_log), whole(d_skip_e),
                  whole(norm_g), whole(e_mat)],
        out_specs=pl.BlockSpec((q, d_inner), lambda b, c: (b * nc + c, 0)),
        out_shape=jax.ShapeDtypeStruct((m, d_inner), BF16),
        scratch_shapes=[
            pltpu.VMEM((slabs, F32_SUBLANES + q, LANES), F32),
            pltpu.VMEM((slabs, q, LANES), F32),
            pltpu.VMEM((SSM_GROUPS, SSM_STATE, gw), F32),
            per_group,
            per_group,
            per_group,
            pltpu.VMEM((SSM_GROUPS, F32_SUBLANES, gw), F32),
            pltpu.VMEM((SSM_GROUPS, q, heads // SSM_GROUPS), F32),
            pltpu.VMEM((heads, q), F32),
            per_group,
        ],
        compiler_params=_params(("arbitrary", "arbitrary")),
        name="ssd",
    )(proj, proj, proj, proj, proj, proj, dt_raw,
      conv_w, conv_b, dt_bias, a_log, d_skip_e, norm_g, e_mat)


def _merge_kernel(yz_ref, gb_ref, ya_ref, x_ref, wssm_ref, wo_ref, g_ref, h_ref, u_ref):
    yb = _dot(yz_ref[...], wssm_ref[...])
    mixed = (ya_ref[...].astype(F32) + jax.nn.sigmoid(gb_ref[...].astype(F32)) * yb).astype(BF16)
    h = x_ref[...] + _dot(mixed, wo_ref[...])
    h_ref[...] = h
    u_ref[...] = _rmsnorm(h, g_ref[...]).astype(BF16)


def _merge(yz, proj, ya, x2, w_ssm, w_o, g_ffn, *, tm, col_gate_b):
    m, d = x2.shape
    di = yz.shape[1]
    return pl.pallas_call(
        _merge_kernel,
        grid=(m // tm,),
        in_specs=[pl.BlockSpec((tm, di), lambda i: (i, 0)),
                  pl.BlockSpec((tm, d), lambda i: (i, col_gate_b)),
                  pl.BlockSpec((tm, d), lambda i: (i, 0)),
                  pl.BlockSpec((tm, d), lambda i: (i, 0)),
                  _resident(w_ssm.shape), _resident(w_o.shape), _resident(g_ffn.shape)],
        out_specs=[pl.BlockSpec((tm, d), lambda i: (i, 0)),
                   pl.BlockSpec((tm, d), lambda i: (i, 0))],
        out_shape=[jax.ShapeDtypeStruct((m, d), F32), jax.ShapeDtypeStruct((m, d), BF16)],
        compiler_params=_params(("arbitrary",)),
        name="merge",
    )(yz, proj, ya, x2, w_ssm, w_o, g_ffn)


def _ffn_kernel(u_ref, h_ref, wg_ref, wu_ref, wd_ref, gf_ref, out_ref, *, final_norm):
    f = pl.program_id(1)
    u = u_ref[...]
    gate = _dot(u, wg_ref[...])
    act = (gate * jax.nn.sigmoid(gate) * _dot(u, wu_ref[...])).astype(BF16)
    part = _dot(act, wd_ref[...])

    @pl.when(f == 0)
    def _():
        out_ref[...] = h_ref[...] + part

    @pl.when(f > 0)
    def _():
        out_ref[...] += part

    if final_norm:
        @pl.when(f == pl.num_programs(1) - 1)
        def _():
            out_ref[...] = _rmsnorm(out_ref[...], gf_ref[...])


def _ffn(u, h, w_gate, w_up, w_down, g_final, *, tm, tf, final_norm):
    m, d = h.shape
    ff = w_gate.shape[1]
    return pl.pallas_call(
        functools.partial(_ffn_kernel, final_norm=final_norm),
        grid=(m // tm, ff // tf),
        in_specs=[pl.BlockSpec((tm, d), lambda i, f: (i, 0)),
                  pl.BlockSpec((tm, d), lambda i, f: (i, 0)),
                  pl.BlockSpec((d, tf), lambda i, f: (0, f)),
                  pl.BlockSpec((d, tf), lambda i, f: (0, f)),
                  pl.BlockSpec((tf, d), lambda i, f: (f, 0)),
                  pl.BlockSpec((1, d), lambda i, f: (0, 0))],
        out_specs=pl.BlockSpec((tm, d), lambda i, f: (i, 0)),
        out_shape=jax.ShapeDtypeStruct((m, d), F32),
        compiler_params=_params(("arbitrary", "arbitrary")),
        name="ffn",
    )(u, h, w_gate, w_up, w_down, g_final)


def _pick(n, candidates):
    for c in candidates:
        if n % c == 0:
            return c
    raise ValueError(f"no tile in {candidates} divides {n}")


def kernel(x, norm_mix_g, w_in, conv_a_w, w_out_a, ssm_conv_w, ssm_conv_b, dt_bias, a_log, d_skip,
           ssm_norm_g, w_out_ssm, w_o, norm_ffn_g, w_ffn_gate, w_ffn_up, w_ffn_down, norm_final_g):
    batch, seq, d = x.shape
    m = batch * seq
    depth = w_in.shape[0]
    d_inner = w_out_ssm.shape[1]
    heads = dt_bias.shape[1]
    gn = SSM_GROUPS * SSM_STATE
    conv_dim = d_inner + 2 * gn
    n_main = w_in.shape[2] - heads
    off_z = 2 * d + 3 * conv_a_w.shape[2]
    off_xbc = off_z + d_inner
    half = d_inner // 2
    assert conv_a_w.shape[2] == d and off_xbc + conv_dim == n_main
    assert seq % SSM_CHUNK == 0 and heads <= LANES and heads % (2 * SSM_GROUPS) == 0
    assert off_z % half == 0 and off_xbc % half == 0 and (off_xbc + d_inner) % gn == 0
    assert (d_inner // SSM_GROUPS) % LANES == 0 and half % (d_inner // SSM_GROUPS) == 0

    hp = d_inner // heads
    e_mat = (lax.broadcasted_iota(jnp.int32, (LANES, d_inner), 0)
             == lax.broadcasted_iota(jnp.int32, (LANES, d_inner), 1) // hp).astype(BF16)
    pad_h = ((0, 0), (0, LANES - heads))

    h = x.reshape(m, d)
    for i in range(depth):
        w_main = w_in[i, :, :n_main].astype(BF16)
        w_dt = jnp.pad(w_in[i, :, n_main:], pad_h).astype(BF16)
        conv_w = ssm_conv_w[i].reshape(-1, conv_dim // LANES, LANES).transpose(1, 0, 2)
        conv_b = ssm_conv_b[i].reshape(conv_dim // LANES, 1, LANES)
        dtb = jnp.pad(dt_bias[i][None], pad_h)
        alog = jnp.pad(a_log[i][None], pad_h)
        d_skip_e = jnp.repeat(d_skip[i], hp).reshape(SSM_GROUPS, 1, d_inner // SSM_GROUPS)

        proj, dt_raw = _in_proj(h, norm_mix_g[i][None], w_main, w_dt,
                                tm=_pick(m, (1024, 512, 256, 128)),
                                tn=_pick(n_main, (1024, 512, 256, 128)))
        ya = _mixer_a(proj, conv_a_w[i], w_out_a[i].astype(BF16), seq=seq,
                      tm=_pick(seq, (512, 256, 128)), col_gate=0, col_b=2)
        yz = _ssd(proj, dt_raw, conv_w, conv_b, dtb, alog, d_skip_e, ssm_norm_g[i][None], e_mat,
                  batch=batch, seq=seq, heads=heads, col_z=off_z // half, col_x=off_xbc // half,
                  col_b=(off_xbc + d_inner) // gn)
        h, u = _merge(yz, proj, ya, h, w_out_ssm[i].astype(BF16), w_o[i].astype(BF16),
                      norm_ffn_g[i][None], tm=_pick(m, (256, 128)), col_gate_b=1)
        h = _ffn(u, h, w_ffn_gate[i].astype(BF16), w_ffn_up[i].astype(BF16),
                 w_ffn_down[i].astype(BF16), norm_final_g[None],
                 tm=_pick(m, (512, 256, 128)), tf=_pick(w_ffn_gate.shape[2], (512, 256, 128)),
                 final_norm=(i == depth - 1))
    return h.reshape(batch, seq, d)
```

```python
import functools

import jax
import jax.numpy as jnp
from jax import lax
from jax.experimental import pallas as pl
from jax.experimental.pallas import tpu as pltpu

EPS = 1e-6
SSM_GROUPS = 8
SSM_STATE = 128
SSM_CHUNK = 128

LANES = 128
BF16_SUBLANES = 16
F32_SUBLANES = 8
V7X_VMEM_LIMIT_BYTES = 60000 * 1024

F32 = jnp.float32
BF16 = jnp.bfloat16


def _params(semantics):
    return pltpu.CompilerParams(dimension_semantics=semantics,
                                vmem_limit_bytes=V7X_VMEM_LIMIT_BYTES)


def _resident(shape):
    return pl.BlockSpec(shape, lambda *_: (0,) * len(shape), pipeline_mode=pl.Buffered(1))


def _rmsnorm(x, g):
    return x * lax.rsqrt(jnp.mean(x * x, axis=-1, keepdims=True) + EPS) * g


def _dot(a, b):
    return jnp.dot(a, b, preferred_element_type=F32)


def _in_proj_kernel(x_ref, g_ref, w_ref, wdt_ref, proj_ref, dt_ref, u_ref):
    @pl.when(pl.program_id(1) == 0)
    def _():
        u = _rmsnorm(x_ref[...], g_ref[...]).astype(BF16)
        u_ref[...] = u
        dt_ref[...] = _dot(u, wdt_ref[...])

    proj_ref[...] = _dot(u_ref[...], w_ref[...]).astype(BF16)


def _in_proj(x2, g, w_all, w_dt, *, n, tm, tn):
    m, d = x2.shape
    return pl.pallas_call(
        _in_proj_kernel,
        grid=(m // tm, n // tn),
        in_specs=[
            pl.BlockSpec((tm, d), lambda i, j: (i, 0)),
            pl.BlockSpec((1, d), lambda i, j: (0, 0)),
            pl.BlockSpec((d, tn), lambda i, j: (0, j)),
            pl.BlockSpec((d, LANES), lambda i, j: (0, 0)),
        ],
        out_specs=[
            pl.BlockSpec((tm, tn), lambda i, j: (i, j)),
            pl.BlockSpec((tm, LANES), lambda i, j: (i, 0)),
        ],
        out_shape=[jax.ShapeDtypeStruct((m, n), BF16),
                   jax.ShapeDtypeStruct((m, LANES), F32)],
        scratch_shapes=[pltpu.VMEM((tm, d), BF16)],
        compiler_params=_params(("arbitrary", "arbitrary")),
        name="in_proj",
    )(x2, g, w_all, w_dt)


def _mixer_a_kernel(ga_ref, b_ref, c_ref, h_ref, ch_ref, hh_ref, cw_ref, wo_ref, out_ref,
                    buf_ref, *, tiles_per_seq):
    tm = c_ref.shape[0]
    halo = ch_ref.shape[0]
    taps = cw_ref.shape[0]
    v = c_ref[...].astype(F32) * h_ref[...].astype(F32)
    vh = ch_ref[...].astype(F32) * hh_ref[...].astype(F32)
    seq_start = pl.program_id(0) % tiles_per_seq == 0
    buf_ref[0:halo, :] = jnp.where(seq_start, 0.0, vh)
    buf_ref[halo:, :] = v
    acc = cw_ref[taps - 1:taps, :] * v
    for j in range(taps - 1):
        lo = halo - (taps - 1 - j)
        acc = acc + cw_ref[j:j + 1, :] * buf_ref[lo:lo + tm, :]
    y = (b_ref[...].astype(F32) * acc).astype(BF16)
    ya = _dot(y, wo_ref[...])
    out_ref[...] = (jax.nn.sigmoid(ga_ref[...].astype(F32)) * ya).astype(BF16)


def _mixer_a(proj, conv_w, w_out, *, seq, tm, col_gate, col_b):
    m = proj.shape[0]
    c = conv_w.shape[1]
    d = w_out.shape[1]
    halo = BF16_SUBLANES
    hb = tm // halo

    def col(k):
        return pl.BlockSpec((tm, c), lambda i: (i, k))

    def halo_col(k):
        return pl.BlockSpec((halo, c), lambda i: (jnp.maximum(i * hb - 1, 0), k))

    return pl.pallas_call(
        functools.partial(_mixer_a_kernel, tiles_per_seq=seq // tm),
        grid=(m // tm,),
        in_specs=[col(col_gate), col(col_b), col(col_b + 1), col(col_b + 2),
                  halo_col(col_b + 1), halo_col(col_b + 2),
                  _resident(conv_w.shape), _resident(w_out.shape)],
        out_specs=pl.BlockSpec((tm, d), lambda i: (i, 0)),
        out_shape=jax.ShapeDtypeStruct((m, d), BF16),
        scratch_shapes=[pltpu.VMEM((halo + tm, c), F32)],
        compiler_params=_params(("arbitrary",)),
        name="mixer_a",
    )(proj, proj, proj, proj, proj, proj, conv_w, w_out)


def _ssd_kernel(z0_ref, z1_ref, x0_ref, x1_ref, b_ref, c_ref, dt_ref,
                cw_ref, cb_ref, dtb_ref, alog_ref, dsk_ref, ng_ref, e_ref,
                out_ref,
                cbuf_ref, xbc_ref, st_ref, dte_ref, dee_ref, ecs_ref, cd_ref, csg_ref, cst_ref,
                y_ref):
    q = SSM_CHUNK
    groups = SSM_GROUPS
    taps = cw_ref.shape[1]
    gw = y_ref.shape[2]
    hg = csg_ref.shape[2]
    hp = gw // hg
    n_heads = cst_ref.shape[0]
    slabs_x = groups * gw // LANES
    slabs_g = gw // LANES
    half = x0_ref.shape[1]
    tail = F32_SUBLANES

    @pl.when(pl.program_id(1) == 0)
    def _():
        cbuf_ref[:, 0:tail, :] = jnp.zeros((cbuf_ref.shape[0], tail, LANES), F32)
        st_ref[...] = jnp.zeros(st_ref.shape, F32)

    for s in range(slabs_x):
        src = x0_ref if s * LANES < half else x1_ref
        off = (s * LANES) % half
        cbuf_ref[s, tail:, :] = src[:, off:off + LANES].astype(F32)
    for s in range(groups):
        cbuf_ref[slabs_x + s, tail:, :] = b_ref[:, s * LANES:(s + 1) * LANES].astype(F32)
        cbuf_ref[slabs_x + groups + s, tail:, :] = c_ref[:, s * LANES:(s + 1) * LANES].astype(F32)

    def conv_slab(s, carry):
        w = cw_ref[s]
        acc = cb_ref[s] + w[taps - 1:taps, :] * cbuf_ref[s, tail:, :]
        for j in range(taps - 1):
            lo = tail - (taps - 1 - j)
            acc = acc + w[j:j + 1, :] * cbuf_ref[s, lo:lo + q, :]
        xbc_ref[s] = acc * jax.nn.sigmoid(acc)
        cbuf_ref[s, 0:tail, :] = cbuf_ref[s, q:q + tail, :]
        return carry

    lax.fori_loop(0, cbuf_ref.shape[0], conv_slab, 0)

    dt = jax.nn.softplus(dt_ref[...] + dtb_ref[...])
    log_a = dt * -jnp.exp(alog_ref[...])
    row = lax.broadcasted_iota(jnp.int32, (q, q), 0)
    col = lax.broadcasted_iota(jnp.int32, (q, q), 1)
    causal = col <= row
    cs = jnp.dot(causal.astype(F32), log_a, preferred_element_type=F32,
                 precision=lax.Precision.HIGHEST)
    total = cs[q - 1:q, :]
    e_mat = e_ref[...]

    def expand(v):
        return _dot(v.astype(BF16), e_mat)

    def expand_exact(v):
        hi = v.astype(BF16)
        r1 = v - hi.astype(F32)
        mid = r1.astype(BF16)
        lo = (r1 - mid.astype(F32)).astype(BF16)
        return _dot(hi, e_mat) + _dot(mid, e_mat) + _dot(lo, e_mat)

    dt_e = expand(dt)
    de_e = expand(jnp.exp(total - cs))
    ec_e = expand(jnp.exp(cs))
    cd_e = expand_exact(jnp.broadcast_to(jnp.exp(total), (tail, LANES)))
    cst_ref[...] = cs.T[0:n_heads, :]
    for g in range(groups):
        sl = slice(g * gw, (g + 1) * gw)
        dte_ref[g] = dt_e[:, sl]
        dee_ref[g] = de_e[:, sl]
        ecs_ref[g] = ec_e[:, sl]
        cd_ref[g] = cd_e[:, sl]
        csg_ref[g] = cs[:, g * hg:(g + 1) * hg]

    pair_w = 2 * hp
    left = lax.broadcasted_iota(jnp.int32, (q, pair_w), 1) < hp

    def group_body(g, carry):
        bg = xbc_ref[slabs_x + g].astype(BF16)
        cg = xbc_ref[slabs_x + groups + g].astype(BF16)
        cbm = lax.dot_general(cg, bg, (((1,), (1,)), ((), ())), preferred_element_type=F32)
        xs = jnp.concatenate([xbc_ref[g * slabs_g + k] for k in range(slabs_g)], axis=1)
        xdt = xs * dte_ref[g]
        st = st_ref[g]
        y_off = _dot(cg, st.astype(BF16)) * ecs_ref[g]
        xw = (xdt * dee_ref[g]).astype(BF16)
        upd = lax.dot_general(bg, xw, (((0,), (0,)), ((), ())), preferred_element_type=F32)
        st_ref[g] = st * cd_ref[g][0:1, :] + upd
        csg = csg_ref[g]
        cstg = cst_ref[pl.ds(pl.multiple_of(g * hg, hg), hg), :]
        base = y_off + dsk_ref[g] * xs
        for p in range(hg // 2):
            ms = []
            for k in (2 * p, 2 * p + 1):
                seg = csg[:, k:k + 1] - cstg[k:k + 1, :]
                decay = jnp.exp(jnp.where(causal, seg, -jnp.inf))
                ms.append((cbm * decay).astype(BF16))
            x2 = xdt[:, p * pair_w:(p + 1) * pair_w]
            rhs = jnp.concatenate([jnp.where(left, x2, 0.0).astype(BF16),
                                   jnp.where(left, 0.0, x2).astype(BF16)], axis=0)
            yd = _dot(jnp.concatenate(ms, axis=1), rhs)
            y_ref[g, :, p * pair_w:(p + 1) * pair_w] = yd + base[:, p * pair_w:(p + 1) * pair_w]
        return carry

    lax.fori_loop(0, groups, group_body, 0)

    for g in range(groups):
        src = z0_ref if g * gw < half else z1_ref
        off = (g * gw) % half
        z = src[:, off:off + gw].astype(F32)
        yz = y_ref[g] * (z * jax.nn.sigmoid(z))
        yz = yz * lax.rsqrt(jnp.mean(yz * yz, axis=-1, keepdims=True) + EPS)
        out_ref[:, g * gw:(g + 1) * gw] = (yz * ng_ref[:, g * gw:(g + 1) * gw]).astype(BF16)


def _ssd(proj, dt_raw, conv_w, conv_b, dt_bias, a_log, d_skip_e, norm_g, e_mat, *,
         batch, seq, heads, col_z, col_x, col_b):
    m = proj.shape[0]
    q = SSM_CHUNK
    nc = seq // q
    d_inner = norm_g.shape[1]
    half = d_inner // 2
    gn = SSM_GROUPS * SSM_STATE
    gw = d_inner // SSM_GROUPS
    slabs = conv_w.shape[0]

    def rows(width, k):
        return pl.BlockSpec((q, width), lambda b, c: (b * nc + c, k))

    def whole(a):
        return pl.BlockSpec(a.shape, lambda b, c: (0,) * a.ndim)

    per_group = pltpu.VMEM((SSM_GROUPS, q, gw), F32)
    return pl.pallas_call(
        _ssd_kernel,
        grid=(batch, nc),
        in_specs=[rows(half, col_z), rows(half, col_z + 1), rows(half, col_x), rows(half, col_x + 1),
                  rows(gn, col_b), rows(gn, col_b + 1), rows(LANES, 0),
                  whole(conv_w), whole(conv_b), whole(dt_bias), whole(a_log), whole(d_skip_e),
                  whole(norm_g), whole(e_mat)],
        out_specs=pl.BlockSpec((q, d_inner), lambda b, c: (b * nc + c, 0)),
        out_shape=jax.ShapeDtypeStruct((m, d_inner), BF16),
        scratch_shapes=[
            pltpu.VMEM((slabs, F32_SUBLANES + q, LANES), F32),
            pltpu.VMEM((slabs, q, LANES), F32),
            pltpu.VMEM((SSM_GROUPS, SSM_STATE, gw), F32),
            per_group,
            per_group,
            per_group,
            pltpu.VMEM((SSM_GROUPS, F32_SUBLANES, gw), F32),
            pltpu.VMEM((SSM_GROUPS, q, heads // SSM_GROUPS), F32),
            pltpu.VMEM((heads, q), F32),
            per_group,
        ],
        compiler_params=_params(("arbitrary", "arbitrary")),
        name="ssd",
    )(proj, proj, proj, proj, proj, proj, dt_raw,
      conv_w, conv_b, dt_bias, a_log, d_skip_e, norm_g, e_mat)


def _merge_kernel(yz_ref, gb_ref, ya_ref, x_ref, wssm_ref, wo_ref, h_ref):
    yb = _dot(yz_ref[...], wssm_ref[...])
    mixed = (ya_ref[...].astype(F32) + jax.nn.sigmoid(gb_ref[...].astype(F32)) * yb).astype(BF16)
    h_ref[...] = x_ref[...] + _dot(mixed, wo_ref[...])


def _merge(yz, proj, ya, x2, w_ssm, w_o, *, tm, col_gate_b):
    m, d = x2.shape
    di = yz.shape[1]
    return pl.pallas_call(
        _merge_kernel,
        grid=(m // tm,),
        in_specs=[pl.BlockSpec((tm, di), lambda i: (i, 0)),
                  pl.BlockSpec((tm, d), lambda i: (i, col_gate_b)),
                  pl.BlockSpec((tm, d), lambda i: (i, 0)),
                  pl.BlockSpec((tm, d), lambda i: (i, 0)),
                  _resident(w_ssm.shape), _resident(w_o.shape)],
        out_specs=pl.BlockSpec((tm, d), lambda i: (i, 0)),
        out_shape=jax.ShapeDtypeStruct((m, d), F32),
        compiler_params=_params(("arbitrary",)),
        name="merge",
    )(yz, proj, ya, x2, w_ssm, w_o)


def _ffn_kernel(h_ref, g_ref, wg_ref, wu_ref, wd_ref, gf_ref, out_ref, u_ref, *, final_norm):
    f = pl.program_id(1)

    @pl.when(f == 0)
    def _():
        h = h_ref[...]
        u_ref[...] = _rmsnorm(h, g_ref[...]).astype(BF16)
        out_ref[...] = h

    u = u_ref[...]
    gate = _dot(u, wg_ref[...])
    act = (gate * jax.nn.sigmoid(gate) * _dot(u, wu_ref[...])).astype(BF16)
    out_ref[...] += _dot(act, wd_ref[...])

    if final_norm:
        @pl.when(f == pl.num_programs(1) - 1)
        def _():
            out_ref[...] = _rmsnorm(out_ref[...], gf_ref[...])


def _ffn(h, g_ffn, w_gate, w_up, w_down, g_final, *, tm, tf, final_norm):
    m, d = h.shape
    ff = w_gate.shape[1]
    return pl.pallas_call(
        functools.partial(_ffn_kernel, final_norm=final_norm),
        grid=(m // tm, ff // tf),
        in_specs=[pl.BlockSpec((tm, d), lambda i, f: (i, 0)),
                  pl.BlockSpec((1, d), lambda i, f: (0, 0)),
                  pl.BlockSpec((d, tf), lambda i, f: (0, f)),
                  pl.BlockSpec((d, tf), lambda i, f: (0, f)),
                  pl.BlockSpec((tf, d), lambda i, f: (f, 0)),
                  pl.BlockSpec((1, d), lambda i, f: (0, 0))],
        out_specs=pl.BlockSpec((tm, d), lambda i, f: (i, 0)),
        out_shape=jax.ShapeDtypeStruct((m, d), F32),
        scratch_shapes=[pltpu.VMEM((tm, d), BF16)],
        compiler_params=_params(("arbitrary", "arbitrary")),
        name="ffn",
    )(h, g_ffn, w_gate, w_up, w_down, g_final)


def _pick(n, candidates):
    for c in candidates:
        if n % c == 0:
            return c
    raise ValueError(f"no tile in {candidates} divides {n}")


def kernel(x, norm_mix_g, w_in, conv_a_w, w_out_a, ssm_conv_w, ssm_conv_b, dt_bias, a_log, d_skip,
           ssm_norm_g, w_out_ssm, w_o, norm_ffn_g, w_ffn_gate, w_ffn_up, w_ffn_down, norm_final_g):
    batch, seq, d = x.shape
    m = batch * seq
    depth = w_in.shape[0]
    d_inner = w_out_ssm.shape[1]
    heads = dt_bias.shape[1]
    gn = SSM_GROUPS * SSM_STATE
    conv_dim = d_inner + 2 * gn
    n_main = w_in.shape[2] - heads
    off_z = 2 * d + 3 * conv_a_w.shape[2]
    off_xbc = off_z + d_inner
    half = d_inner // 2
    assert conv_a_w.shape[2] == d and off_xbc + conv_dim == n_main
    assert seq % SSM_CHUNK == 0 and heads <= LANES and heads % (2 * SSM_GROUPS) == 0
    assert off_z % half == 0 and off_xbc % half == 0 and (off_xbc + d_inner) % gn == 0
    assert (d_inner // SSM_GROUPS) % LANES == 0 and half % (d_inner // SSM_GROUPS) == 0

    hp = d_inner // heads
    e_mat = (lax.broadcasted_iota(jnp.int32, (LANES, d_inner), 0)
             == lax.broadcasted_iota(jnp.int32, (LANES, d_inner), 1) // hp).astype(BF16)
    pad_h = ((0, 0), (0, LANES - heads))

    h = x.reshape(m, d)
    for i in range(depth):
        w_all = w_in[i].astype(BF16)
        w_dt = jnp.pad(w_in[i, :, n_main:], pad_h).astype(BF16)
        conv_w = ssm_conv_w[i].reshape(-1, conv_dim // LANES, LANES).transpose(1, 0, 2)
        conv_b = ssm_conv_b[i].reshape(conv_dim // LANES, 1, LANES)
        dtb = jnp.pad(dt_bias[i][None], pad_h)
        alog = jnp.pad(a_log[i][None], pad_h)
        d_skip_e = jnp.repeat(d_skip[i], hp).reshape(SSM_GROUPS, 1, d_inner // SSM_GROUPS)

        proj, dt_raw = _in_proj(h, norm_mix_g[i][None], w_all, w_dt, n=n_main,
                                tm=_pick(m, (1024, 512, 256, 128)),
                                tn=_pick(n_main, (2048, 1024, 512, 256, 128)))
        ya = _mixer_a(proj, conv_a_w[i], w_out_a[i].astype(BF16), seq=seq,
                      tm=_pick(seq, (512, 256, 128)), col_gate=0, col_b=2)
        yz = _ssd(proj, dt_raw, conv_w, conv_b, dtb, alog, d_skip_e, ssm_norm_g[i][None], e_mat,
                  batch=batch, seq=seq, heads=heads, col_z=off_z // half, col_x=off_xbc // half,
                  col_b=(off_xbc + d_inner) // gn)
        h = _merge(yz, proj, ya, h, w_out_ssm[i].astype(BF16), w_o[i].astype(BF16),
                   tm=_pick(m, (256, 128)), col_gate_b=1)
        h = _ffn(h, norm_ffn_g[i][None], w_ffn_gate[i].astype(BF16), w_ffn_up[i].astype(BF16),
                 w_ffn_down[i].astype(BF16), norm_final_g[None],
                 tm=_pick(m, (1024, 512, 256, 128)), tf=_pick(w_ffn_gate.shape[2], (512, 256, 128)),
                 final_norm=(i == depth - 1))
    return h.reshape(batch, seq, d)
```

```python
import functools

import jax
import jax.numpy as jnp
from jax import lax
from jax.experimental import pallas as pl
from jax.experimental.pallas import tpu as pltpu

EPS = 1e-6
LOG2_E = 1.4426950408889634
SSM_GROUPS = 8
SSM_STATE = 128
SSM_CHUNK = 128

LANES = 128
BF16_SUBLANES = 16
F32_SUBLANES = 8
V7X_VMEM_LIMIT_BYTES = 60000 * 1024

F32 = jnp.float32
BF16 = jnp.bfloat16


def _params(semantics):
    return pltpu.CompilerParams(dimension_semantics=semantics,
                                vmem_limit_bytes=V7X_VMEM_LIMIT_BYTES)


def _resident(shape):
    return pl.BlockSpec(shape, lambda *_: (0,) * len(shape), pipeline_mode=pl.Buffered(1))


def _rmsnorm(x, g):
    return x * lax.rsqrt(jnp.mean(x * x, axis=-1, keepdims=True) + EPS) * g


def _dot(a, b):
    return jnp.dot(a, b, preferred_element_type=F32)


def _sigmoid(x):
    return 1.0 / (1.0 + jnp.exp2(x * -LOG2_E))


def _store_projection(res, out_ref, epilogue):
    if epilogue == "sigmoid":
        res = _sigmoid(res)
    elif epilogue == "silu":
        res = res * _sigmoid(res)
    out_ref[...] = res.astype(BF16)


def _norm_proj_kernel(x_ref, g_ref, w_ref, wdt_ref, out_ref, dt_ref, u_ref, *, epilogue):
    @pl.when(pl.program_id(1) == 0)
    def _():
        u = _rmsnorm(x_ref[...], g_ref[...]).astype(BF16)
        u_ref[...] = u
        dt_ref[...] = _dot(u, wdt_ref[...])

    _store_projection(_dot(u_ref[...], w_ref[...]), out_ref, epilogue)


def _norm_proj(x2, g, w_all, w_dt, *, col0, n_tiles, tm, tn, epilogue):
    m, d = x2.shape
    return pl.pallas_call(
        functools.partial(_norm_proj_kernel, epilogue=epilogue),
        grid=(m // tm, n_tiles),
        in_specs=[
            pl.BlockSpec((tm, d), lambda i, j: (i, 0)),
            pl.BlockSpec((1, d), lambda i, j: (0, 0)),
            pl.BlockSpec((d, tn), lambda i, j: (0, col0 + j)),
            pl.BlockSpec((d, LANES), lambda i, j: (0, 0)),
        ],
        out_specs=[
            pl.BlockSpec((tm, tn), lambda i, j: (i, j)),
            pl.BlockSpec((tm, LANES), lambda i, j: (i, 0)),
            pl.BlockSpec((tm, d), lambda i, j: (i, 0)),
        ],
        out_shape=[jax.ShapeDtypeStruct((m, n_tiles * tn), BF16),
                   jax.ShapeDtypeStruct((m, LANES), F32),
                   jax.ShapeDtypeStruct((m, d), BF16)],
        compiler_params=_params(("arbitrary", "arbitrary")),
        name="norm_proj_" + epilogue,
    )(x2, g, w_all, w_dt)


def _proj_kernel(u_ref, w_ref, out_ref, *, epilogue):
    _store_projection(_dot(u_ref[...], w_ref[...]), out_ref, epilogue)


def _proj(u, w_all, *, col0, n_tiles, tm, tn, epilogue):
    m, d = u.shape
    return pl.pallas_call(
        functools.partial(_proj_kernel, epilogue=epilogue),
        grid=(m // tm, n_tiles),
        in_specs=[pl.BlockSpec((tm, d), lambda i, j: (i, 0)),
                  pl.BlockSpec((d, tn), lambda i, j: (0, col0 + j))],
        out_specs=pl.BlockSpec((tm, tn), lambda i, j: (i, j)),
        out_shape=jax.ShapeDtypeStruct((m, n_tiles * tn), BF16),
        compiler_params=_params(("arbitrary", "arbitrary")),
        name="proj_" + epilogue,
    )(u, w_all)


def _proj_conv_kernel(u_ref, w_ref, cw_ref, cb_ref, out_ref, buf_ref, carry_ref, *, tiles_per_seq):
    tm = u_ref.shape[0]
    taps = cw_ref.shape[0]
    tail = F32_SUBLANES
    j = pl.program_id(1)
    seq_start = pl.program_id(0) % tiles_per_seq == 0
    res = _dot(u_ref[...], w_ref[...])
    for c in range(buf_ref.shape[0]):
        lanes = slice(c * LANES, (c + 1) * LANES)
        res_c = res[:, lanes]
        buf_ref[c, 0:tail, :] = jnp.where(seq_start, 0.0, carry_ref[j, c])
        buf_ref[c, tail:, :] = res_c
        acc = cb_ref[:, lanes] + cw_ref[taps - 1:taps, lanes] * res_c
        for k in range(taps - 1):
            lo = tail - (taps - 1 - k)
            acc = acc + cw_ref[k:k + 1, lanes] * buf_ref[c, lo:lo + tm, :]
        carry_ref[j, c] = buf_ref[c, tm:tm + tail, :]
        out_ref[:, lanes] = (acc * _sigmoid(acc)).astype(BF16)


def _proj_conv(u, w_all, conv_w, conv_b, *, col0, n_tiles, seq, tm, tn):
    m, d = u.shape
    taps = conv_w.shape[0]
    return pl.pallas_call(
        functools.partial(_proj_conv_kernel, tiles_per_seq=seq // tm),
        grid=(m // tm, n_tiles),
        in_specs=[pl.BlockSpec((tm, d), lambda i, j: (i, 0)),
                  pl.BlockSpec((d, tn), lambda i, j: (0, col0 + j)),
                  pl.BlockSpec((taps, tn), lambda i, j: (0, j)),
                  pl.BlockSpec((1, tn), lambda i, j: (0, j))],
        out_specs=pl.BlockSpec((tm, tn), lambda i, j: (i, j)),
        out_shape=jax.ShapeDtypeStruct((m, n_tiles * tn), BF16),
        scratch_shapes=[pltpu.VMEM((tn // LANES, F32_SUBLANES + tm, LANES), F32),
                        pltpu.VMEM((n_tiles, tn // LANES, F32_SUBLANES, LANES), F32)],
        compiler_params=_params(("arbitrary", "arbitrary")),
        name="proj_conv",
    )(u, w_all, conv_w, conv_b)


def _mixer_a_kernel(ga_ref, b_ref, c_ref, h_ref, ch_ref, hh_ref, cw_ref, wo_ref, out_ref,
                    buf_ref, y_ref, *, tiles_per_seq):
    tm = c_ref.shape[0]
    halo = ch_ref.shape[0]
    taps = cw_ref.shape[0]
    seq_start = pl.program_id(0) % tiles_per_seq == 0
    for s in range(buf_ref.shape[0]):
        lanes = slice(s * LANES, (s + 1) * LANES)
        v = c_ref[:, lanes].astype(F32) * h_ref[:, lanes].astype(F32)
        vh = ch_ref[:, lanes].astype(F32) * hh_ref[:, lanes].astype(F32)
        buf_ref[s, 0:halo, :] = jnp.where(seq_start, 0.0, vh)
        buf_ref[s, halo:, :] = v
        acc = cw_ref[taps - 1:taps, lanes] * v
        for j in range(taps - 1):
            lo = halo - (taps - 1 - j)
            acc = acc + cw_ref[j:j + 1, lanes] * buf_ref[s, lo:lo + tm, :]
        y_ref[:, lanes] = (b_ref[:, lanes].astype(F32) * acc).astype(BF16)
    out_ref[...] = (ga_ref[...].astype(F32) * _dot(y_ref[...], wo_ref[...])).astype(BF16)


def _mixer_a(gates, pa, conv_w, w_out, *, seq, tm):
    m = pa.shape[0]
    c = conv_w.shape[1]
    d = w_out.shape[1]
    halo = BF16_SUBLANES
    hb = tm // halo

    def col(k):
        return pl.BlockSpec((tm, c), lambda i: (i, k))

    def halo_col(k):
        return pl.BlockSpec((halo, c), lambda i: (jnp.maximum(i * hb - 1, 0), k))

    return pl.pallas_call(
        functools.partial(_mixer_a_kernel, tiles_per_seq=seq // tm),
        grid=(m // tm,),
        in_specs=[col(0), col(0), col(1), col(2), halo_col(1), halo_col(2),
                  _resident(conv_w.shape), _resident(w_out.shape)],
        out_specs=pl.BlockSpec((tm, d), lambda i: (i, 0)),
        out_shape=jax.ShapeDtypeStruct((m, d), BF16),
        scratch_shapes=[pltpu.VMEM((c // LANES, halo + tm, LANES), F32),
                        pltpu.VMEM((tm, c), BF16)],
        compiler_params=_params(("arbitrary",)),
        name="mixer_a",
    )(gates, pa, pa, pa, pa, pa, conv_w, w_out)


def _ssd_kernel(xs_ref, bc_ref, dt_ref, dtb_ref, alog_ref, dsk_ref, y_ref,
                st_ref, cs_ref, rt_ref, wt_ref):
    q = SSM_CHUNK
    groups = SSM_GROUPS
    slabs_x = st_ref.shape[0]
    pairs_g = slabs_x // groups
    hp = LANES // 2

    @pl.when(pl.program_id(1) == 0)
    def _():
        st_ref[...] = jnp.zeros(st_ref.shape, F32)

    dt = jax.nn.softplus(dt_ref[...] + dtb_ref[...])
    log_a = dt * -jnp.exp(alog_ref[...])
    row = lax.broadcasted_iota(jnp.int32, (q, q), 0)
    col = lax.broadcasted_iota(jnp.int32, (q, q), 1)
    causal = col <= row
    cs = LOG2_E * jnp.dot(causal.astype(F32), log_a, preferred_element_type=F32,
                          precision=lax.Precision.HIGHEST)
    r = cs - LOG2_E * jnp.log(dt)
    cs_ref[...] = cs
    rt_ref[...] = r.T
    wt_ref[...] = jnp.exp2(cs[q - 1:q, :] - r).T

    left = lax.broadcasted_iota(jnp.int32, (q, LANES), 1) < hp
    for g in range(groups):
        bg = bc_ref[:, g * LANES:(g + 1) * LANES]
        cg = bc_ref[:, (groups + g) * LANES:(groups + g + 1) * LANES]
        cbm = lax.dot_general(cg, bg, (((1,), (1,)), ((), ())),
                              preferred_element_type=F32).astype(BF16)
        bt = bg.T.reshape(SSM_STATE // BF16_SUBLANES, BF16_SUBLANES, q)
        st_g = jnp.concatenate([st_ref[g * pairs_g + p] for p in range(pairs_g)], axis=1)
        y_off = _dot(cg, st_g.astype(BF16))
        for p in range(pairs_g):
            slab = g * pairs_g + p
            lanes = slice(slab * LANES, (slab + 1) * LANES)
            lhs_y, lhs_s, a = [], [], []
            for h in (2 * slab, 2 * slab + 1):
                a_h = cs_ref[:, h:h + 1]
                a.append(a_h)
                seg = a_h - rt_ref[h:h + 1, :]
                decay = jnp.exp2(jnp.where(causal, seg, -jnp.inf))
                lhs_y.append(cbm * decay.astype(BF16))
                w_h = jnp.broadcast_to(wt_ref[h:h + 1, :], (BF16_SUBLANES, q)).astype(BF16)
                lhs_s.append((bt * w_h[None]).reshape(SSM_STATE, q))
            lhs = jnp.concatenate([jnp.concatenate(lhs_y, axis=1),
                                   jnp.concatenate(lhs_s, axis=1)], axis=0)
            xs = xs_ref[:, lanes].astype(F32)
            rhs = jnp.concatenate([jnp.where(left, xs, 0.0).astype(BF16),
                                   jnp.where(left, 0.0, xs).astype(BF16)], axis=0)
            res = _dot(lhs, rhs)
            e12 = jnp.exp2(jnp.where(left, a[0], a[1]))
            st_ref[slab] = st_ref[slab] * e12[q - 1:q, :] + res[q:, :]
            y = res[:q, :] + y_off[:, p * LANES:(p + 1) * LANES] * e12 + dsk_ref[:, lanes] * xs
            y_ref[:, lanes] = y.astype(BF16)


def _ssd(xbc, dt_raw, dt_bias, a_log, d_skip_e, *, batch, seq, d_inner):
    m = xbc.shape[0]
    q = SSM_CHUNK
    nc = seq // q
    w_bc = xbc.shape[1] - d_inner
    assert d_inner % w_bc == 0

    def rows(width, k=0):
        return pl.BlockSpec((q, width), lambda b, c: (b * nc + c, k))

    def whole(a):
        return pl.BlockSpec(a.shape, lambda b, c: (0,) * a.ndim)

    return pl.pallas_call(
        _ssd_kernel,
        grid=(batch, nc),
        in_specs=[rows(d_inner), rows(w_bc, d_inner // w_bc), rows(LANES),
                  whole(dt_bias), whole(a_log), whole(d_skip_e)],
        out_specs=rows(d_inner),
        out_shape=jax.ShapeDtypeStruct((m, d_inner), BF16),
        scratch_shapes=[
            pltpu.VMEM((d_inner // LANES, SSM_STATE, LANES), F32),
            pltpu.VMEM((q, LANES), F32),
            pltpu.VMEM((LANES, q), F32),
            pltpu.VMEM((LANES, q), F32),
        ],
        compiler_params=_params(("arbitrary", "arbitrary")),
        name="ssd",
    )(xbc, xbc, dt_raw, dt_bias, a_log, d_skip_e)


def _merge_kernel(y_ref, za_ref, gb_ref, ya_ref, x_ref, ng_ref, wssm_ref, wo_ref, h_ref, yz_ref):
    gw = y_ref.shape[1] // SSM_GROUPS
    for g in range(SSM_GROUPS):
        cols = slice(g * gw, (g + 1) * gw)
        yz = y_ref[:, cols].astype(F32) * za_ref[:, cols].astype(F32)
        yz = yz * lax.rsqrt(jnp.mean(yz * yz, axis=-1, keepdims=True) + EPS)
        yz_ref[:, cols] = (yz * ng_ref[:, cols]).astype(BF16)

    yb = _dot(yz_ref[...], wssm_ref[...])
    mixed = (ya_ref[...].astype(F32) + gb_ref[...].astype(F32) * yb).astype(BF16)
    h_ref[...] = x_ref[...] + _dot(mixed, wo_ref[...])


def _merge(y, z_act, gates, ya, x2, norm_g, w_ssm, w_o, *, tm):
    m, d = x2.shape
    di = y.shape[1]
    return pl.pallas_call(
        _merge_kernel,
        grid=(m // tm,),
        in_specs=[pl.BlockSpec((tm, di), lambda i: (i, 0)),
                  pl.BlockSpec((tm, di), lambda i: (i, 0)),
                  pl.BlockSpec((tm, d), lambda i: (i, 1)),
                  pl.BlockSpec((tm, d), lambda i: (i, 0)),
                  pl.BlockSpec((tm, d), lambda i: (i, 0)),
                  _resident(norm_g.shape), _resident(w_ssm.shape), _resident(w_o.shape)],
        out_specs=pl.BlockSpec((tm, d), lambda i: (i, 0)),
        out_shape=jax.ShapeDtypeStruct((m, d), F32),
        scratch_shapes=[pltpu.VMEM((tm, di), BF16)],
        compiler_params=_params(("arbitrary",)),
        name="merge",
    )(y, z_act, gates, ya, x2, norm_g, w_ssm, w_o)


def _ffn_kernel(h_ref, g_ref, wg_ref, wu_ref, wd_ref, gf_ref, out_ref, u_ref, *, final_norm):
    f = pl.program_id(1)

    @pl.when(f == 0)
    def _():
        h = h_ref[...]
        u_ref[...] = _rmsnorm(h, g_ref[...]).astype(BF16)
        out_ref[...] = h

    u = u_ref[...]
    gate = _dot(u, wg_ref[...])
    act = (gate * _sigmoid(gate) * _dot(u, wu_ref[...])).astype(BF16)
    out_ref[...] += _dot(act, wd_ref[...])

    if final_norm:
        @pl.when(f == pl.num_programs(1) - 1)
        def _():
            out_ref[...] = _rmsnorm(out_ref[...], gf_ref[...])


def _ffn(h, g_ffn, w_gate, w_up, w_down, g_final, *, tm, tf, final_norm):
    m, d = h.shape
    ff = w_gate.shape[1]
    return pl.pallas_call(
        functools.partial(_ffn_kernel, final_norm=final_norm),
        grid=(m // tm, ff // tf),
        in_specs=[pl.BlockSpec((tm, d), lambda i, f: (i, 0)),
                  pl.BlockSpec((1, d), lambda i, f: (0, 0)),
                  pl.BlockSpec((d, tf), lambda i, f: (0, f)),
                  pl.BlockSpec((d, tf), lambda i, f: (0, f)),
                  pl.BlockSpec((tf, d), lambda i, f: (f, 0)),
                  pl.BlockSpec((1, d), lambda i, f: (0, 0))],
        out_specs=pl.BlockSpec((tm, d), lambda i, f: (i, 0)),
        out_shape=jax.ShapeDtypeStruct((m, d), F32),
        scratch_shapes=[pltpu.VMEM((tm, d), BF16)],
        compiler_params=_params(("arbitrary", "arbitrary")),
        name="ffn",
    )(h, g_ffn, w_gate, w_up, w_down, g_final)


def _pick(n, candidates):
    for c in candidates:
        if n % c == 0:
            return c
    raise ValueError(f"no tile in {candidates} divides {n}")


def kernel(x, norm_mix_g, w_in, conv_a_w, w_out_a, ssm_conv_w, ssm_conv_b, dt_bias, a_log, d_skip,
           ssm_norm_g, w_out_ssm, w_o, norm_ffn_g, w_ffn_gate, w_ffn_up, w_ffn_down, norm_final_g):
    batch, seq, d = x.shape
    m = batch * seq
    depth = w_in.shape[0]
    d_inner = w_out_ssm.shape[1]
    heads = dt_bias.shape[1]
    gn = SSM_GROUPS * SSM_STATE
    conv_dim = d_inner + 2 * gn
    n_main = w_in.shape[2] - heads
    off_a = 2 * d
    off_z = off_a + 3 * conv_a_w.shape[2]
    off_xbc = off_z + d_inner
    hp = d_inner // heads
    tn = d
    assert conv_a_w.shape[2] == d and off_xbc + conv_dim == n_main
    assert d_inner % tn == 0 and conv_dim % tn == 0 and (2 * gn) % tn == 0
    assert seq % SSM_CHUNK == 0 and heads <= LANES and heads % (2 * SSM_GROUPS) == 0
    assert 2 * hp == LANES and SSM_STATE == LANES and SSM_CHUNK == LANES
    pad_h = ((0, 0), (0, LANES - heads))
    tm_proj = _pick(seq, (1024, 512, 256, 128))

    h = x.reshape(m, d)
    for i in range(depth):
        w_all = w_in[i].astype(BF16)
        w_dt = jnp.pad(w_in[i, :, n_main:], pad_h).astype(BF16)
        dtb = jnp.pad(dt_bias[i][None], pad_h)
        alog = jnp.pad(a_log[i][None], pad_h)
        d_skip_e = jnp.repeat(d_skip[i], hp)[None]

        gates, dt_raw, u = _norm_proj(h, norm_mix_g[i][None], w_all, w_dt, col0=0, n_tiles=off_a // tn,
                                      tm=tm_proj, tn=tn, epilogue="sigmoid")
        pa = _proj(u, w_all, col0=off_a // tn, n_tiles=(off_z - off_a) // tn,
                   tm=tm_proj, tn=tn, epilogue="none")
        z_act = _proj(u, w_all, col0=off_z // tn, n_tiles=d_inner // tn,
                      tm=tm_proj, tn=tn, epilogue="silu")
        xbc = _proj_conv(u, w_all, ssm_conv_w[i], ssm_conv_b[i][None], col0=off_xbc // tn,
                         n_tiles=conv_dim // tn, seq=seq, tm=tm_proj, tn=tn)

        ya = _mixer_a(gates, pa, conv_a_w[i], w_out_a[i].astype(BF16), seq=seq,
                      tm=_pick(seq, (512, 256, 128)))
        y = _ssd(xbc, dt_raw, dtb, alog, d_skip_e, batch=batch, seq=seq, d_inner=d_inner)
        h = _merge(y, z_act, gates, ya, h, ssm_norm_g[i][None], w_out_ssm[i].astype(BF16),
                   w_o[i].astype(BF16), tm=_pick(m, (256, 128)))
        h = _ffn(h, norm_ffn_g[i][None], w_ffn_gate[i].astype(BF16), w_ffn_up[i].astype(BF16),
                 w_ffn_down[i].astype(BF16), norm_final_g[None],
                 tm=_pick(m, (1024, 512, 256, 128)), tf=_pick(w_ffn_gate.shape[2], (512, 256, 128)),
                 final_norm=(i == depth - 1))
    return h.reshape(batch, seq, d)
```

```python
import functools

import jax
import jax.numpy as jnp
from jax import lax
from jax.experimental import pallas as pl
from jax.experimental.pallas import tpu as pltpu

EPS = 1e-6
LOG2_E = 1.4426950408889634
SSM_GROUPS = 8
SSM_STATE = 128
SSM_CHUNK = 128

LANES = 128
BF16_SUBLANES = 16
F32_SUBLANES = 8
V7X_VMEM_LIMIT_BYTES = 60000 * 1024

F32 = jnp.float32
BF16 = jnp.bfloat16


def _params(semantics):
    return pltpu.CompilerParams(dimension_semantics=semantics,
                                vmem_limit_bytes=V7X_VMEM_LIMIT_BYTES)


def _resident(shape):
    return pl.BlockSpec(shape, lambda *_: (0,) * len(shape), pipeline_mode=pl.Buffered(1))


def _rmsnorm(x, g):
    return x * lax.rsqrt(jnp.mean(x * x, axis=-1, keepdims=True) + EPS) * g


def _dot(a, b):
    return jnp.dot(a, b, preferred_element_type=F32)


def _sigmoid(x):
    return 1.0 / (1.0 + jnp.exp2(x * -LOG2_E))


def _store_projection(res, out_ref, epilogue):
    if epilogue == "silu":
        res = res * _sigmoid(res)
    else:
        assert epilogue == "none"
    out_ref[...] = res.astype(BF16)


def _tile_major(m, n_tiles, tm, tn):
    return (jax.ShapeDtypeStruct((n_tiles, m, tn), BF16),
            pl.BlockSpec((None, tm, tn), lambda i, j: (j, i, 0)))


def _norm_proj_kernel(x_ref, g_ref, w_ref, wdt_ref, out_ref, dt_ref, u_ref, *, epilogue):
    @pl.when(pl.program_id(1) == 0)
    def _():
        u = _rmsnorm(x_ref[...], g_ref[...]).astype(BF16)
        u_ref[...] = u
        dt_ref[...] = _dot(u, wdt_ref[...])

    _store_projection(_dot(u_ref[...], w_ref[...]), out_ref, epilogue)


def _norm_proj(x2, g, w_all, w_dt, *, col0, n_tiles, tm, tn, epilogue):
    m, d = x2.shape
    out_shape, out_spec = _tile_major(m, n_tiles, tm, tn)
    return pl.pallas_call(
        functools.partial(_norm_proj_kernel, epilogue=epilogue),
        grid=(m // tm, n_tiles),
        in_specs=[
            pl.BlockSpec((tm, d), lambda i, j: (i, 0)),
            pl.BlockSpec((1, d), lambda i, j: (0, 0)),
            pl.BlockSpec((d, tn), lambda i, j: (0, col0 + j)),
            pl.BlockSpec((d, LANES), lambda i, j: (0, 0)),
        ],
        out_specs=[
            out_spec,
            pl.BlockSpec((tm, LANES), lambda i, j: (i, 0)),
            pl.BlockSpec((tm, d), lambda i, j: (i, 0)),
        ],
        out_shape=[out_shape,
                   jax.ShapeDtypeStruct((m, LANES), F32),
                   jax.ShapeDtypeStruct((m, d), BF16)],
        compiler_params=_params(("arbitrary", "arbitrary")),
        name="norm_proj_" + epilogue,
    )(x2, g, w_all, w_dt)


def _proj_kernel(u_ref, w_ref, out_ref, *, epilogue):
    _store_projection(_dot(u_ref[...], w_ref[...]), out_ref, epilogue)


def _proj(u, w_all, *, col0, n_tiles, tm, tn, epilogue):
    m, d = u.shape
    out_shape, out_spec = _tile_major(m, n_tiles, tm, tn)
    return pl.pallas_call(
        functools.partial(_proj_kernel, epilogue=epilogue),
        grid=(m // tm, n_tiles),
        in_specs=[pl.BlockSpec((tm, d), lambda i, j: (i, 0)),
                  pl.BlockSpec((d, tn), lambda i, j: (0, col0 + j))],
        out_specs=out_spec,
        out_shape=out_shape,
        compiler_params=_params(("arbitrary", "arbitrary")),
        name="proj_" + epilogue,
    )(u, w_all)


def _proj_conv_kernel(u_ref, w_ref, cw_ref, cb_ref, out_ref, buf_ref, carry_ref, *, tiles_per_seq):
    tm = u_ref.shape[0]
    taps = cw_ref.shape[0]
    tail = F32_SUBLANES
    j = pl.program_id(1)
    seq_start = pl.program_id(0) % tiles_per_seq == 0
    res = _dot(u_ref[...], w_ref[...])
    for c in range(buf_ref.shape[0]):
        lanes = slice(c * LANES, (c + 1) * LANES)
        res_c = res[:, lanes]
        buf_ref[c, 0:tail, :] = jnp.where(seq_start, 0.0, carry_ref[j, c])
        buf_ref[c, tail:, :] = res_c
        acc = cb_ref[:, lanes] + cw_ref[taps - 1:taps, lanes] * res_c
        for k in range(taps - 1):
            lo = tail - (taps - 1 - k)
            acc = acc + cw_ref[k:k + 1, lanes] * buf_ref[c, lo:lo + tm, :]
        carry_ref[j, c] = buf_ref[c, tm:tm + tail, :]
        out_ref[:, lanes] = (acc * _sigmoid(acc)).astype(BF16)


def _proj_conv(u, w_all, conv_w, conv_b, *, col0, n_tiles, seq, tm, tn):
    m, d = u.shape
    taps = conv_w.shape[0]
    out_shape, out_spec = _tile_major(m, n_tiles, tm, tn)
    return pl.pallas_call(
        functools.partial(_proj_conv_kernel, tiles_per_seq=seq // tm),
        grid=(m // tm, n_tiles),
        in_specs=[pl.BlockSpec((tm, d), lambda i, j: (i, 0)),
                  pl.BlockSpec((d, tn), lambda i, j: (0, col0 + j)),
                  pl.BlockSpec((taps, tn), lambda i, j: (0, j)),
                  pl.BlockSpec((1, tn), lambda i, j: (0, j))],
        out_specs=out_spec,
        out_shape=out_shape,
        scratch_shapes=[pltpu.VMEM((tn // LANES, F32_SUBLANES + tm, LANES), F32),
                        pltpu.VMEM((n_tiles, tn // LANES, F32_SUBLANES, LANES), F32)],
        compiler_params=_params(("arbitrary", "arbitrary")),
        name="proj_conv",
    )(u, w_all, conv_w, conv_b)


def _mixer_a_kernel(ga_ref, b_ref, c_ref, h_ref, ch_ref, hh_ref, cw_ref, wo_ref, out_ref,
                    buf_ref, y_ref, *, tiles_per_seq):
    tm = c_ref.shape[0]
    halo = ch_ref.shape[0]
    taps = cw_ref.shape[0]
    seq_start = pl.program_id(0) % tiles_per_seq == 0
    for s in range(buf_ref.shape[0]):
        lanes = slice(s * LANES, (s + 1) * LANES)
        v = c_ref[:, lanes].astype(F32) * h_ref[:, lanes].astype(F32)
        vh = ch_ref[:, lanes].astype(F32) * hh_ref[:, lanes].astype(F32)
        buf_ref[s, 0:halo, :] = jnp.where(seq_start, 0.0, vh)
        buf_ref[s, halo:, :] = v
        acc = cw_ref[taps - 1:taps, lanes] * v
        for j in range(taps - 1):
            lo = halo - (taps - 1 - j)
            acc = acc + cw_ref[j:j + 1, lanes] * buf_ref[s, lo:lo + tm, :]
        y_ref[:, lanes] = (b_ref[:, lanes].astype(F32) * acc).astype(BF16)
    out_ref[...] = (_sigmoid(ga_ref[...].astype(F32)) * _dot(y_ref[...], wo_ref[...])).astype(BF16)


def _mixer_a(pa, conv_w, w_out, *, seq, tm, tile_gate, tile_b):
    m = pa.shape[1]
    c = conv_w.shape[1]
    d = w_out.shape[1]
    halo = BF16_SUBLANES
    hb = tm // halo

    def col(k):
        return pl.BlockSpec((None, tm, c), lambda i: (k, i, 0))

    def halo_col(k):
        return pl.BlockSpec((None, halo, c), lambda i: (k, jnp.maximum(i * hb - 1, 0), 0))

    return pl.pallas_call(
        functools.partial(_mixer_a_kernel, tiles_per_seq=seq // tm),
        grid=(m // tm,),
        in_specs=[col(tile_gate), col(tile_b), col(tile_b + 1), col(tile_b + 2),
                  halo_col(tile_b + 1), halo_col(tile_b + 2),
                  _resident(conv_w.shape), _resident(w_out.shape)],
        out_specs=pl.BlockSpec((tm, d), lambda i: (i, 0)),
        out_shape=jax.ShapeDtypeStruct((m, d), BF16),
        scratch_shapes=[pltpu.VMEM((c // LANES, halo + tm, LANES), F32),
                        pltpu.VMEM((tm, c), BF16)],
        compiler_params=_params(("arbitrary",)),
        name="mixer_a",
    )(pa, pa, pa, pa, pa, pa, conv_w, w_out)


def _ssd_prep_kernel(dt_ref, dtb_ref, alog_ref, cs_ref, rt_ref, wt_ref):
    q = SSM_CHUNK
    row = lax.broadcasted_iota(jnp.int32, (q, q), 0)
    col = lax.broadcasted_iota(jnp.int32, (q, q), 1)
    tril = (col <= row).astype(F32)
    a = -jnp.exp(alog_ref[...])
    for k in range(rt_ref.shape[0]):
        rows = slice(k * q, (k + 1) * q)
        dt = jax.nn.softplus(dt_ref[rows, :] + dtb_ref[...])
        cs = LOG2_E * jnp.dot(tril, dt * a, preferred_element_type=F32,
                              precision=lax.Precision.HIGHEST)
        r = cs - LOG2_E * jnp.log(dt)
        cs_ref[rows, :] = cs
        rt_ref[k] = r.T
        wt_ref[k] = jnp.exp2(cs[q - 1:q, :] - r).T


def _ssd_prep(dt_raw, dt_bias, a_log, *, tr):
    m = dt_raw.shape[0]
    q = SSM_CHUNK
    per_chunk = jax.ShapeDtypeStruct((m // q, LANES, q), F32)
    return pl.pallas_call(
        _ssd_prep_kernel,
        grid=(m // tr,),
        in_specs=[pl.BlockSpec((tr, LANES), lambda i: (i, 0)),
                  pl.BlockSpec(dt_bias.shape, lambda i: (0, 0)),
                  pl.BlockSpec(a_log.shape, lambda i: (0, 0))],
        out_specs=[pl.BlockSpec((tr, LANES), lambda i: (i, 0)),
                   pl.BlockSpec((tr // q, LANES, q), lambda i: (i, 0, 0)),
                   pl.BlockSpec((tr // q, LANES, q), lambda i: (i, 0, 0))],
        out_shape=[jax.ShapeDtypeStruct((m, LANES), F32), per_chunk, per_chunk],
        compiler_params=_params(("arbitrary",)),
        name="ssd_prep",
    )(dt_raw, dt_bias, a_log)


def _ssd_kernel(x0_ref, x1_ref, bc_ref, cs_ref, rt_ref, wt_ref, dsk_ref, y_ref, st_ref):
    q = SSM_CHUNK
    groups = SSM_GROUPS
    slabs_x = st_ref.shape[0]
    pairs_g = slabs_x // groups
    hp = LANES // 2
    half = x0_ref.shape[1]

    @pl.when(pl.program_id(1) == 0)
    def _():
        st_ref[...] = jnp.zeros(st_ref.shape, F32)

    row = lax.broadcasted_iota(jnp.int32, (q, q), 0)
    col = lax.broadcasted_iota(jnp.int32, (q, q), 1)
    causal = col <= row
    left = lax.broadcasted_iota(jnp.int32, (q, LANES), 1) < hp
    for g in range(groups):
        bg = bc_ref[:, g * LANES:(g + 1) * LANES]
        cg = bc_ref[:, (groups + g) * LANES:(groups + g + 1) * LANES]
        cbm = lax.dot_general(cg, bg, (((1,), (1,)), ((), ())),
                              preferred_element_type=F32).astype(BF16)
        bt = bg.T.reshape(SSM_STATE // BF16_SUBLANES, BF16_SUBLANES, q)
        st_g = jnp.concatenate([st_ref[g * pairs_g + p] for p in range(pairs_g)], axis=1)
        y_off = _dot(cg, st_g.astype(BF16))
        for p in range(pairs_g):
            slab = g * pairs_g + p
            lanes = slice(slab * LANES, (slab + 1) * LANES)
            lhs_y, lhs_s, a = [], [], []
            for h in (2 * slab, 2 * slab + 1):
                a_h = cs_ref[:, h:h + 1]
                a.append(a_h)
                seg = a_h - rt_ref[0, h:h + 1, :]
                decay = jnp.exp2(jnp.where(causal, seg, -jnp.inf))
                lhs_y.append(cbm * decay.astype(BF16))
                w_h = jnp.broadcast_to(wt_ref[0, h:h + 1, :], (BF16_SUBLANES, q)).astype(BF16)
                lhs_s.append((bt * w_h[None]).reshape(SSM_STATE, q))
            lhs = jnp.concatenate([jnp.concatenate(lhs_y, axis=1),
                                   jnp.concatenate(lhs_s, axis=1)], axis=0)
            src = x0_ref if slab * LANES < half else x1_ref
            off = (slab * LANES) % half
            xs = src[:, off:off + LANES].astype(F32)
            rhs = jnp.concatenate([jnp.where(left, xs, 0.0).astype(BF16),
                                   jnp.where(left, 0.0, xs).astype(BF16)], axis=0)
            res = _dot(lhs, rhs)
            e12 = jnp.exp2(jnp.where(left, a[0], a[1]))
            st_ref[slab] = st_ref[slab] * e12[q - 1:q, :] + res[q:, :]
            y = res[:q, :] + y_off[:, p * LANES:(p + 1) * LANES] * e12 + dsk_ref[:, lanes] * xs
            y_ref[:, lanes] = y.astype(BF16)


def _ssd(xbc, cs, rt, wt, d_skip_e, *, batch, seq):
    n_tiles, m, tn = xbc.shape
    assert n_tiles == 3 and tn == 2 * SSM_GROUPS * SSM_STATE
    d_inner = 2 * tn
    q = SSM_CHUNK
    nc = seq // q

    def tile(k):
        return pl.BlockSpec((None, q, tn), lambda b, c: (k, b * nc + c, 0))

    def rows(width):
        return pl.BlockSpec((q, width), lambda b, c: (b * nc + c, 0))

    per_chunk = pl.BlockSpec((1, LANES, q), lambda b, c: (b * nc + c, 0, 0))
    return pl.pallas_call(
        _ssd_kernel,
        grid=(batch, nc),
        in_specs=[tile(0), tile(1), tile(2), rows(LANES), per_chunk, per_chunk,
                  pl.BlockSpec(d_skip_e.shape, lambda b, c: (0, 0))],
        out_specs=rows(d_inner),
        out_shape=jax.ShapeDtypeStruct((m, d_inner), BF16),
        scratch_shapes=[pltpu.VMEM((d_inner // LANES, SSM_STATE, LANES), F32)],
        compiler_params=_params(("arbitrary", "arbitrary")),
        name="ssd",
    )(xbc, xbc, xbc, cs, rt, wt, d_skip_e)


def _merge_kernel(y_ref, z0_ref, z1_ref, gb_ref, ya_ref, x_ref, ng_ref, wssm_ref, wo_ref, h_ref,
                  yz_ref):
    gw = y_ref.shape[1] // SSM_GROUPS
    half = z0_ref.shape[1]
    for g in range(SSM_GROUPS):
        cols = slice(g * gw, (g + 1) * gw)
        src = z0_ref if g * gw < half else z1_ref
        off = (g * gw) % half
        yz = y_ref[:, cols].astype(F32) * src[:, off:off + gw].astype(F32)
        yz = yz * lax.rsqrt(jnp.mean(yz * yz, axis=-1, keepdims=True) + EPS)
        yz_ref[:, cols] = (yz * ng_ref[:, cols]).astype(BF16)

    yb = _dot(yz_ref[...], wssm_ref[...])
    mixed = (ya_ref[...].astype(F32) + _sigmoid(gb_ref[...].astype(F32)) * yb).astype(BF16)
    h_ref[...] = x_ref[...] + _dot(mixed, wo_ref[...])


def _merge(y, z_act, pa, ya, x2, norm_g, w_ssm, w_o, *, tm, tile_gate_b):
    m, d = x2.shape
    di = y.shape[1]
    assert z_act.shape == (2, m, di // 2) and pa.shape[2] == d

    def tile(width, k):
        return pl.BlockSpec((None, tm, width), lambda i: (k, i, 0))

    def rows(width):
        return pl.BlockSpec((tm, width), lambda i: (i, 0))

    return pl.pallas_call(
        _merge_kernel,
        grid=(m // tm,),
        in_specs=[rows(di), tile(di // 2, 0), tile(di // 2, 1), tile(d, tile_gate_b), rows(d), rows(d),
                  _resident(norm_g.shape), _resident(w_ssm.shape), _resident(w_o.shape)],
        out_specs=rows(d),
        out_shape=jax.ShapeDtypeStruct((m, d), F32),
        scratch_shapes=[pltpu.VMEM((tm, di), BF16)],
        compiler_params=_params(("arbitrary",)),
        name="merge",
    )(y, z_act, z_act, pa, ya, x2, norm_g, w_ssm, w_o)


def _ffn_kernel(h_ref, g_ref, wg_ref, wu_ref, wd_ref, gf_ref, out_ref, u_ref, *, final_norm):
    f = pl.program_id(1)

    @pl.when(f == 0)
    def _():
        h = h_ref[...]
        u_ref[...] = _rmsnorm(h, g_ref[...]).astype(BF16)
        out_ref[...] = h

    u = u_ref[...]
    gate = _dot(u, wg_ref[...])
    act = (gate * _sigmoid(gate) * _dot(u, wu_ref[...])).astype(BF16)
    out_ref[...] += _dot(act, wd_ref[...])

    if final_norm:
        @pl.when(f == pl.num_programs(1) - 1)
        def _():
            out_ref[...] = _rmsnorm(out_ref[...], gf_ref[...])


def _ffn(h, g_ffn, w_gate, w_up, w_down, g_final, *, tm, tf, final_norm):
    m, d = h.shape
    ff = w_gate.shape[1]
    return pl.pallas_call(
        functools.partial(_ffn_kernel, final_norm=final_norm),
        grid=(m // tm, ff // tf),
        in_specs=[pl.BlockSpec((tm, d), lambda i, f: (i, 0)),
                  pl.BlockSpec((1, d), lambda i, f: (0, 0)),
                  pl.BlockSpec((d, tf), lambda i, f: (0, f)),
                  pl.BlockSpec((d, tf), lambda i, f: (0, f)),
                  pl.BlockSpec((tf, d), lambda i, f: (f, 0)),
                  pl.BlockSpec((1, d), lambda i, f: (0, 0))],
        out_specs=pl.BlockSpec((tm, d), lambda i, f: (i, 0)),
        out_shape=jax.ShapeDtypeStruct((m, d), F32),
        scratch_shapes=[pltpu.VMEM((tm, d), BF16)],
        compiler_params=_params(("arbitrary", "arbitrary")),
        name="ffn",
    )(h, g_ffn, w_gate, w_up, w_down, g_final)


def _pick(n, candidates):
    for c in candidates:
        if n % c == 0:
            return c
    raise ValueError(f"no tile in {candidates} divides {n}")


def kernel(x, norm_mix_g, w_in, conv_a_w, w_out_a, ssm_conv_w, ssm_conv_b, dt_bias, a_log, d_skip,
           ssm_norm_g, w_out_ssm, w_o, norm_ffn_g, w_ffn_gate, w_ffn_up, w_ffn_down, norm_final_g):
    batch, seq, d = x.shape
    m = batch * seq
    depth = w_in.shape[0]
    d_inner = w_out_ssm.shape[1]
    heads = dt_bias.shape[1]
    gn = SSM_GROUPS * SSM_STATE
    conv_dim = d_inner + 2 * gn
    n_main = w_in.shape[2] - heads
    off_a = 2 * d
    off_z = off_a + 3 * conv_a_w.shape[2]
    off_xbc = off_z + d_inner
    hp = d_inner // heads
    tn = d
    assert conv_a_w.shape[2] == d and off_xbc + conv_dim == n_main
    assert d_inner == 2 * tn and 2 * gn == tn
    assert seq % SSM_CHUNK == 0 and heads <= LANES and heads % (2 * SSM_GROUPS) == 0
    assert 2 * hp == LANES and SSM_STATE == LANES and SSM_CHUNK == LANES
    pad_h = ((0, 0), (0, LANES - heads))
    tm_proj = _pick(seq, (1024, 512, 256, 128))

    h = x.reshape(m, d)
    for i in range(depth):
        w_all = w_in[i].astype(BF16)
        w_dt = jnp.pad(w_in[i, :, n_main:], pad_h).astype(BF16)
        dtb = jnp.pad(dt_bias[i][None], pad_h)
        alog = jnp.pad(a_log[i][None], pad_h)
        d_skip_e = jnp.repeat(d_skip[i], hp)[None]

        pa, dt_raw, u = _norm_proj(h, norm_mix_g[i][None], w_all, w_dt, col0=0, n_tiles=off_z // tn,
                                   tm=tm_proj, tn=tn, epilogue="none")
        z_act = _proj(u, w_all, col0=off_z // tn, n_tiles=d_inner // tn,
                      tm=tm_proj, tn=tn, epilogue="silu")
        xbc = _proj_conv(u, w_all, ssm_conv_w[i], ssm_conv_b[i][None], col0=off_xbc // tn,
                         n_tiles=conv_dim // tn, seq=seq, tm=tm_proj, tn=tn)

        ya = _mixer_a(pa, conv_a_w[i], w_out_a[i].astype(BF16), seq=seq,
                      tm=_pick(seq, (512, 256, 128)), tile_gate=0, tile_b=off_a // tn)
        cs, rt, wt = _ssd_prep(dt_raw, dtb, alog, tr=tm_proj)
        y = _ssd(xbc, cs, rt, wt, d_skip_e, batch=batch, seq=seq)
        h = _merge(y, z_act, pa, ya, h, ssm_norm_g[i][None], w_out_ssm[i].astype(BF16),
                   w_o[i].astype(BF16), tm=_pick(m, (256, 128)), tile_gate_b=1)
        h = _ffn(h, norm_ffn_g[i][None], w_ffn_gate[i].astype(BF16), w_ffn_up[i].astype(BF16),
                 w_ffn_down[i].astype(BF16), norm_final_g[None],
                 tm=_pick(m, (1024, 512, 256, 128)), tf=_pick(w_ffn_gate.shape[2], (512, 256, 128)),
                 final_norm=(i == depth - 1))
    return h.reshape(batch, seq, d)
```

```python
import functools

import jax
import jax.numpy as jnp
from jax import lax
from jax.experimental import pallas as pl
from jax.experimental.pallas import tpu as pltpu

EPS = 1e-6
LOG2_E = 1.4426950408889634
SSM_GROUPS = 8
SSM_STATE = 128
SSM_CHUNK = 128
SSD_CHUNKS_PER_STEP = 4

LANES = 128
BF16_SUBLANES = 16
F32_SUBLANES = 8
V7X_VMEM_LIMIT_BYTES = 60000 * 1024

F32 = jnp.float32
BF16 = jnp.bfloat16


def _params(semantics):
    return pltpu.CompilerParams(dimension_semantics=semantics,
                                vmem_limit_bytes=V7X_VMEM_LIMIT_BYTES)


def _resident(shape):
    return pl.BlockSpec(shape, lambda *_: (0,) * len(shape), pipeline_mode=pl.Buffered(1))


def _rmsnorm(x, g):
    return x * lax.rsqrt(jnp.mean(x * x, axis=-1, keepdims=True) + EPS) * g


def _dot(a, b):
    return jnp.dot(a, b, preferred_element_type=F32)


def _sigmoid(x):
    return 1.0 / (1.0 + jnp.exp2(x * -LOG2_E))


CAST_BLOCK_BYTES = 8 * 1024 * 1024


def _cast_kernel(w_ref, o_ref):
    o_ref[...] = w_ref[...].astype(BF16)


def _rows_to_bf16(w2d, *, row0, rows):
    cols = w2d.shape[1]
    tr = BF16_SUBLANES
    while rows % (2 * tr) == 0 and row0 % (2 * tr) == 0 and 2 * tr * cols * 4 <= CAST_BLOCK_BYTES:
        tr *= 2
    assert rows % tr == 0 and row0 % tr == 0
    return pl.pallas_call(
        _cast_kernel,
        grid=(rows // tr,),
        in_specs=[pl.BlockSpec((tr, cols), lambda r: (row0 // tr + r, 0))],
        out_specs=pl.BlockSpec((tr, cols), lambda r: (r, 0)),
        out_shape=jax.ShapeDtypeStruct((rows, cols), BF16),
        compiler_params=_params(("arbitrary",)),
        name="rows_to_bf16",
    )(w2d)


def _store_projection(res, out_ref, epilogue):
    if epilogue == "silu":
        res = res * _sigmoid(res)
    else:
        assert epilogue == "none"
    out_ref[...] = res.astype(BF16)


def _tile_major(m, n_tiles, tm, tn):
    return (jax.ShapeDtypeStruct((n_tiles, m, tn), BF16),
            pl.BlockSpec((None, tm, tn), lambda i, j: (j, i, 0)))


def _norm_proj_kernel(x_ref, g_ref, w_ref, wdt_ref, out_ref, dt_ref, u_ref, *, epilogue):
    @pl.when(pl.program_id(1) == 0)
    def _():
        u = _rmsnorm(x_ref[...], g_ref[...]).astype(BF16)
        u_ref[...] = u
        dt_ref[...] = _dot(u, wdt_ref[...])

    _store_projection(_dot(u_ref[...], w_ref[...]), out_ref, epilogue)


def _norm_proj(x2, g, w_all, w_dt, *, col0, n_tiles, tm, tn, epilogue):
    m, d = x2.shape
    out_shape, out_spec = _tile_major(m, n_tiles, tm, tn)
    return pl.pallas_call(
        functools.partial(_norm_proj_kernel, epilogue=epilogue),
        grid=(m // tm, n_tiles),
        in_specs=[
            pl.BlockSpec((tm, d), lambda i, j: (i, 0)),
            pl.BlockSpec((1, d), lambda i, j: (0, 0)),
            pl.BlockSpec((d, tn), lambda i, j: (0, col0 + j)),
            pl.BlockSpec((d, LANES), lambda i, j: (0, 0)),
        ],
        out_specs=[
            out_spec,
            pl.BlockSpec((tm, LANES), lambda i, j: (i, 0)),
            pl.BlockSpec((tm, d), lambda i, j: (i, 0)),
        ],
        out_shape=[out_shape,
                   jax.ShapeDtypeStruct((m, LANES), F32),
                   jax.ShapeDtypeStruct((m, d), BF16)],
        compiler_params=_params(("arbitrary", "arbitrary")),
        name="norm_proj_" + epilogue,
    )(x2, g, w_all, w_dt)


def _proj_kernel(u_ref, w_ref, out_ref, *, epilogue):
    _store_projection(_dot(u_ref[...], w_ref[...]), out_ref, epilogue)


def _proj(u, w_all, *, col0, n_tiles, tm, tn, epilogue):
    m, d = u.shape
    out_shape, out_spec = _tile_major(m, n_tiles, tm, tn)
    return pl.pallas_call(
        functools.partial(_proj_kernel, epilogue=epilogue),
        grid=(m // tm, n_tiles),
        in_specs=[pl.BlockSpec((tm, d), lambda i, j: (i, 0)),
                  pl.BlockSpec((d, tn), lambda i, j: (0, col0 + j))],
        out_specs=out_spec,
        out_shape=out_shape,
        compiler_params=_params(("arbitrary", "arbitrary")),
        name="proj_" + epilogue,
    )(u, w_all)


def _proj_conv_kernel(u_ref, w_ref, cw_ref, cb_ref, out_ref, buf_ref, carry_ref, *, tiles_per_seq):
    tm = u_ref.shape[0]
    taps = cw_ref.shape[0]
    tail = F32_SUBLANES
    j = pl.program_id(1)
    seq_start = pl.program_id(0) % tiles_per_seq == 0
    res = _dot(u_ref[...], w_ref[...])
    for c in range(buf_ref.shape[0]):
        lanes = slice(c * LANES, (c + 1) * LANES)
        res_c = res[:, lanes]
        buf_ref[c, 0:tail, :] = jnp.where(seq_start, 0.0, carry_ref[j, c])
        buf_ref[c, tail:, :] = res_c
        acc = cb_ref[:, lanes] + cw_ref[taps - 1:taps, lanes] * res_c
        for k in range(taps - 1):
            lo = tail - (taps - 1 - k)
            acc = acc + cw_ref[k:k + 1, lanes] * buf_ref[c, lo:lo + tm, :]
        carry_ref[j, c] = buf_ref[c, tm:tm + tail, :]
        out_ref[:, lanes] = (acc * _sigmoid(acc)).astype(BF16)


def _proj_conv(u, w_all, conv_w, conv_b, *, col0, n_tiles, seq, tm, tn):
    m, d = u.shape
    taps = conv_w.shape[0]
    out_shape, out_spec = _tile_major(m, n_tiles, tm, tn)
    return pl.pallas_call(
        functools.partial(_proj_conv_kernel, tiles_per_seq=seq // tm),
        grid=(m // tm, n_tiles),
        in_specs=[pl.BlockSpec((tm, d), lambda i, j: (i, 0)),
                  pl.BlockSpec((d, tn), lambda i, j: (0, col0 + j)),
                  pl.BlockSpec((taps, tn), lambda i, j: (0, j)),
                  pl.BlockSpec((1, tn), lambda i, j: (0, j))],
        out_specs=out_spec,
        out_shape=out_shape,
        scratch_shapes=[pltpu.VMEM((tn // LANES, F32_SUBLANES + tm, LANES), F32),
                        pltpu.VMEM((n_tiles, tn // LANES, F32_SUBLANES, LANES), F32)],
        compiler_params=_params(("arbitrary", "arbitrary")),
        name="proj_conv",
    )(u, w_all, conv_w, conv_b)


def _mixer_a_kernel(ga_ref, b_ref, c_ref, h_ref, ch_ref, hh_ref, cw_ref, wo_ref, out_ref,
                    buf_ref, y_ref, *, tiles_per_seq):
    tm = c_ref.shape[0]
    halo = ch_ref.shape[0]
    taps = cw_ref.shape[0]
    seq_start = pl.program_id(0) % tiles_per_seq == 0
    for s in range(buf_ref.shape[0]):
        lanes = slice(s * LANES, (s + 1) * LANES)
        v = c_ref[:, lanes].astype(F32) * h_ref[:, lanes].astype(F32)
        vh = ch_ref[:, lanes].astype(F32) * hh_ref[:, lanes].astype(F32)
        buf_ref[s, 0:halo, :] = jnp.where(seq_start, 0.0, vh)
        buf_ref[s, halo:, :] = v
        acc = cw_ref[taps - 1:taps, lanes] * v
        for j in range(taps - 1):
            lo = halo - (taps - 1 - j)
            acc = acc + cw_ref[j:j + 1, lanes] * buf_ref[s, lo:lo + tm, :]
        y_ref[:, lanes] = (b_ref[:, lanes].astype(F32) * acc).astype(BF16)
    out_ref[...] = (_sigmoid(ga_ref[...].astype(F32)) * _dot(y_ref[...], wo_ref[...])).astype(BF16)


def _mixer_a(pa, conv_w, w_out, *, seq, tm, tile_gate, tile_b):
    m = pa.shape[1]
    c = conv_w.shape[1]
    d = w_out.shape[1]
    halo = BF16_SUBLANES
    hb = tm // halo

    def col(k):
        return pl.BlockSpec((None, tm, c), lambda i: (k, i, 0))

    def halo_col(k):
        return pl.BlockSpec((None, halo, c), lambda i: (k, jnp.maximum(i * hb - 1, 0), 0))

    return pl.pallas_call(
        functools.partial(_mixer_a_kernel, tiles_per_seq=seq // tm),
        grid=(m // tm,),
        in_specs=[col(tile_gate), col(tile_b), col(tile_b + 1), col(tile_b + 2),
                  halo_col(tile_b + 1), halo_col(tile_b + 2),
                  _resident(conv_w.shape), _resident(w_out.shape)],
        out_specs=pl.BlockSpec((tm, d), lambda i: (i, 0)),
        out_shape=jax.ShapeDtypeStruct((m, d), BF16),
        scratch_shapes=[pltpu.VMEM((c // LANES, halo + tm, LANES), F32),
                        pltpu.VMEM((tm, c), BF16)],
        compiler_params=_params(("arbitrary",)),
        name="mixer_a",
    )(pa, pa, pa, pa, pa, pa, conv_w, w_out)


def _ssd_prep_kernel(dt_ref, dtb_ref, alog_ref, cs_ref, rt_ref, wt_ref):
    q = SSM_CHUNK
    row = lax.broadcasted_iota(jnp.int32, (q, q), 0)
    col = lax.broadcasted_iota(jnp.int32, (q, q), 1)
    tril = (col <= row).astype(F32)
    a = -jnp.exp(alog_ref[...])
    for k in range(rt_ref.shape[0]):
        rows = slice(k * q, (k + 1) * q)
        dt = jax.nn.softplus(dt_ref[rows, :] + dtb_ref[...])
        cs = LOG2_E * jnp.dot(tril, dt * a, preferred_element_type=F32,
                              precision=lax.Precision.HIGHEST)
        r = cs - LOG2_E * jnp.log(dt)
        cs_ref[rows, :] = cs
        rt_ref[k] = r.T
        wt_ref[k] = jnp.exp2(cs[q - 1:q, :] - r).T


def _ssd_prep(dt_raw, dt_bias, a_log, *, tr):
    m = dt_raw.shape[0]
    q = SSM_CHUNK
    per_chunk = jax.ShapeDtypeStruct((m // q, LANES, q), F32)
    return pl.pallas_call(
        _ssd_prep_kernel,
        grid=(m // tr,),
        in_specs=[pl.BlockSpec((tr, LANES), lambda i: (i, 0)),
                  pl.BlockSpec(dt_bias.shape, lambda i: (0, 0)),
                  pl.BlockSpec(a_log.shape, lambda i: (0, 0))],
        out_specs=[pl.BlockSpec((tr, LANES), lambda i: (i, 0)),
                   pl.BlockSpec((tr // q, LANES, q), lambda i: (i, 0, 0)),
                   pl.BlockSpec((tr // q, LANES, q), lambda i: (i, 0, 0))],
        out_shape=[jax.ShapeDtypeStruct((m, LANES), F32), per_chunk, per_chunk],
        compiler_params=_params(("arbitrary",)),
        name="ssd_prep",
    )(dt_raw, dt_bias, a_log)


def _ssd_kernel(x0_ref, x1_ref, bc_ref, cs_ref, rt_ref, wt_ref, dsk_ref, y_ref, st_ref):
    q = SSM_CHUNK
    hp = LANES // 2

    @pl.when(pl.program_id(1) == 0)
    def _():
        st_ref[...] = jnp.zeros(st_ref.shape, F32)

    row = lax.broadcasted_iota(jnp.int32, (q, q), 0)
    col = lax.broadcasted_iota(jnp.int32, (q, q), 1)
    causal = col <= row
    left = lax.broadcasted_iota(jnp.int32, (q, LANES), 1) < hp
    for k in range(rt_ref.shape[0]):
        _ssd_chunk(k, x0_ref, x1_ref, bc_ref, cs_ref, rt_ref, wt_ref, dsk_ref, y_ref, st_ref,
                   causal, left)


def _ssd_chunk(k, x0_ref, x1_ref, bc_ref, cs_ref, rt_ref, wt_ref, dsk_ref, y_ref, st_ref, causal, left):
    q = SSM_CHUNK
    groups = SSM_GROUPS
    pairs_g = st_ref.shape[0] // groups
    half = x0_ref.shape[1]
    rows = slice(k * q, (k + 1) * q)
    for g in range(groups):
        bg = bc_ref[rows, g * LANES:(g + 1) * LANES]
        cg = bc_ref[rows, (groups + g) * LANES:(groups + g + 1) * LANES]
        cbm = lax.dot_general(cg, bg, (((1,), (1,)), ((), ())),
                              preferred_element_type=F32).astype(BF16)
        bt = bg.T.reshape(SSM_STATE // BF16_SUBLANES, BF16_SUBLANES, q)
        st_g = jnp.concatenate([st_ref[g * pairs_g + p] for p in range(pairs_g)], axis=1)
        y_off = _dot(cg, st_g.astype(BF16))
        for p in range(pairs_g):
            slab = g * pairs_g + p
            lanes = slice(slab * LANES, (slab + 1) * LANES)
            lhs_y, lhs_s, a = [], [], []
            for h in (2 * slab, 2 * slab + 1):
                a_h = cs_ref[rows, h:h + 1]
                a.append(a_h)
                seg = a_h - rt_ref[k, h:h + 1, :]
                decay = jnp.exp2(jnp.where(causal, seg, -jnp.inf))
                lhs_y.append(cbm * decay.astype(BF16))
                w_h = jnp.broadcast_to(wt_ref[k, h:h + 1, :], (BF16_SUBLANES, q)).astype(BF16)
                lhs_s.append((bt * w_h[None]).reshape(SSM_STATE, q))
            lhs = jnp.concatenate([jnp.concatenate(lhs_y, axis=1),
                                   jnp.concatenate(lhs_s, axis=1)], axis=0)
            src = x0_ref if slab * LANES < half else x1_ref
            off = (slab * LANES) % half
            xs = src[rows, off:off + LANES].astype(F32)
            rhs = jnp.concatenate([jnp.where(left, xs, 0.0).astype(BF16),
                                   jnp.where(left, 0.0, xs).astype(BF16)], axis=0)
            res = _dot(lhs, rhs)
            e12 = jnp.exp2(jnp.where(left, a[0], a[1]))
            st_ref[slab] = st_ref[slab] * e12[q - 1:q, :] + res[q:, :]
            y = res[:q, :] + y_off[:, p * LANES:(p + 1) * LANES] * e12 + dsk_ref[:, lanes] * xs
            y_ref[rows, lanes] = y.astype(BF16)


def _ssd(xbc, cs, rt, wt, d_skip_e, *, batch, seq, chunks_per_step):
    n_tiles, m, tn = xbc.shape
    assert n_tiles == 3 and tn == 2 * SSM_GROUPS * SSM_STATE
    d_inner = 2 * tn
    q = SSM_CHUNK
    tr = chunks_per_step * q
    ns = seq // tr

    def tile(k):
        return pl.BlockSpec((None, tr, tn), lambda b, c: (k, b * ns + c, 0))

    def rows(width):
        return pl.BlockSpec((tr, width), lambda b, c: (b * ns + c, 0))

    per_chunk = pl.BlockSpec((chunks_per_step, LANES, q), lambda b, c: (b * ns + c, 0, 0))
    return pl.pallas_call(
        _ssd_kernel,
        grid=(batch, ns),
        in_specs=[tile(0), tile(1), tile(2), rows(LANES), per_chunk, per_chunk,
                  pl.BlockSpec(d_skip_e.shape, lambda b, c: (0, 0))],
        out_specs=rows(d_inner),
        out_shape=jax.ShapeDtypeStruct((m, d_inner), BF16),
        scratch_shapes=[pltpu.VMEM((d_inner // LANES, SSM_STATE, LANES), F32)],
        compiler_params=_params(("arbitrary", "arbitrary")),
        name="ssd",
    )(xbc, xbc, xbc, cs, rt, wt, d_skip_e)


def _merge_kernel(y_ref, z0_ref, z1_ref, gb_ref, ya_ref, x_ref, ng_ref, wssm_ref, wo_ref, h_ref,
                  yz_ref):
    gw = y_ref.shape[1] // SSM_GROUPS
    half = z0_ref.shape[1]
    for g in range(SSM_GROUPS):
        cols = slice(g * gw, (g + 1) * gw)
        src = z0_ref if g * gw < half else z1_ref
        off = (g * gw) % half
        yz = y_ref[:, cols].astype(F32) * src[:, off:off + gw].astype(F32)
        yz = yz * lax.rsqrt(jnp.mean(yz * yz, axis=-1, keepdims=True) + EPS)
        yz_ref[:, cols] = (yz * ng_ref[:, cols]).astype(BF16)

    yb = _dot(yz_ref[...], wssm_ref[...])
    mixed = (ya_ref[...].astype(F32) + _sigmoid(gb_ref[...].astype(F32)) * yb).astype(BF16)
    h_ref[...] = x_ref[...] + _dot(mixed, wo_ref[...])


def _merge(y, z_act, pa, ya, x2, norm_g, w_ssm, w_o, *, tm, tile_gate_b):
    m, d = x2.shape
    di = y.shape[1]
    assert z_act.shape == (2, m, di // 2) and pa.shape[2] == d

    def tile(width, k):
        return pl.BlockSpec((None, tm, width), lambda i: (k, i, 0))

    def rows(width):
        return pl.BlockSpec((tm, width), lambda i: (i, 0))

    return pl.pallas_call(
        _merge_kernel,
        grid=(m // tm,),
        in_specs=[rows(di), tile(di // 2, 0), tile(di // 2, 1), tile(d, tile_gate_b), rows(d), rows(d),
                  _resident(norm_g.shape), _resident(w_ssm.shape), _resident(w_o.shape)],
        out_specs=rows(d),
        out_shape=jax.ShapeDtypeStruct((m, d), F32),
        scratch_shapes=[pltpu.VMEM((tm, di), BF16)],
        compiler_params=_params(("arbitrary",)),
        name="merge",
    )(y, z_act, z_act, pa, ya, x2, norm_g, w_ssm, w_o)


def _ffn_kernel(h_ref, g_ref, wg_ref, wu_ref, wd_ref, gf_ref, out_ref, u_ref, *, final_norm):
    f = pl.program_id(1)

    @pl.when(f == 0)
    def _():
        h = h_ref[...]
        u_ref[...] = _rmsnorm(h, g_ref[...]).astype(BF16)
        out_ref[...] = h

    u = u_ref[...]
    gate = _dot(u, wg_ref[...])
    act = (gate * _sigmoid(gate) * _dot(u, wu_ref[...])).astype(BF16)
    out_ref[...] += _dot(act, wd_ref[...])

    if final_norm:
        @pl.when(f == pl.num_programs(1) - 1)
        def _():
            out_ref[...] = _rmsnorm(out_ref[...], gf_ref[...])


def _ffn(h, g_ffn, w_gate, w_up, w_down, g_final, *, tm, tf, final_norm):
    m, d = h.shape
    ff = w_gate.shape[1]
    return pl.pallas_call(
        functools.partial(_ffn_kernel, final_norm=final_norm),
        grid=(m // tm, ff // tf),
        in_specs=[pl.BlockSpec((tm, d), lambda i, f: (i, 0)),
                  pl.BlockSpec((1, d), lambda i, f: (0, 0)),
                  pl.BlockSpec((d, tf), lambda i, f: (0, f)),
                  pl.BlockSpec((d, tf), lambda i, f: (0, f)),
                  pl.BlockSpec((tf, d), lambda i, f: (f, 0)),
                  pl.BlockSpec((1, d), lambda i, f: (0, 0))],
        out_specs=pl.BlockSpec((tm, d), lambda i, f: (i, 0)),
        out_shape=jax.ShapeDtypeStruct((m, d), F32),
        scratch_shapes=[pltpu.VMEM((tm, d), BF16)],
        compiler_params=_params(("arbitrary", "arbitrary")),
        name="ffn",
    )(h, g_ffn, w_gate, w_up, w_down, g_final)


def _pick(n, candidates):
    for c in candidates:
        if n % c == 0:
            return c
    raise ValueError(f"no tile in {candidates} divides {n}")


def kernel(x, norm_mix_g, w_in, conv_a_w, w_out_a, ssm_conv_w, ssm_conv_b, dt_bias, a_log, d_skip,
           ssm_norm_g, w_out_ssm, w_o, norm_ffn_g, w_ffn_gate, w_ffn_up, w_ffn_down, norm_final_g):
    batch, seq, d = x.shape
    m = batch * seq
    depth = w_in.shape[0]
    d_inner = w_out_ssm.shape[1]
    heads = dt_bias.shape[1]
    gn = SSM_GROUPS * SSM_STATE
    conv_dim = d_inner + 2 * gn
    n_main = w_in.shape[2] - heads
    off_a = 2 * d
    off_z = off_a + 3 * conv_a_w.shape[2]
    off_xbc = off_z + d_inner
    hp = d_inner // heads
    tn = d
    assert conv_a_w.shape[2] == d and off_xbc + conv_dim == n_main
    assert d_inner == 2 * tn and 2 * gn == tn
    assert seq % SSM_CHUNK == 0 and heads <= LANES and heads % (2 * SSM_GROUPS) == 0
    assert 2 * hp == LANES and SSM_STATE == LANES and SSM_CHUNK == LANES
    pad_h = ((0, 0), (0, LANES - heads))
    tm_proj = _pick(seq, (1024, 512, 256, 128))

    h = x.reshape(m, d)
    for i in range(depth):
        w_all = _rows_to_bf16(w_in.reshape(depth * d, w_in.shape[2]), row0=i * d, rows=d)
        w_dt = jnp.pad(w_in[i, :, n_main:], pad_h).astype(BF16)
        dtb = jnp.pad(dt_bias[i][None], pad_h)
        alog = jnp.pad(a_log[i][None], pad_h)
        d_skip_e = jnp.repeat(d_skip[i], hp)[None]

        pa, dt_raw, u = _norm_proj(h, norm_mix_g[i][None], w_all, w_dt, col0=0, n_tiles=off_z // tn,
                                   tm=tm_proj, tn=tn, epilogue="none")
        z_act = _proj(u, w_all, col0=off_z // tn, n_tiles=d_inner // tn,
                      tm=tm_proj, tn=tn, epilogue="silu")
        xbc = _proj_conv(u, w_all, ssm_conv_w[i], ssm_conv_b[i][None], col0=off_xbc // tn,
                         n_tiles=conv_dim // tn, seq=seq, tm=tm_proj, tn=tn)

        ya = _mixer_a(pa, conv_a_w[i], w_out_a[i].astype(BF16), seq=seq,
                      tm=_pick(seq, (512, 256, 128)), tile_gate=0, tile_b=off_a // tn)
        cs, rt, wt = _ssd_prep(dt_raw, dtb, alog, tr=tm_proj)
        y = _ssd(xbc, cs, rt, wt, d_skip_e, batch=batch, seq=seq,
                 chunks_per_step=_pick(seq // SSM_CHUNK, (SSD_CHUNKS_PER_STEP, 2, 1)))
        h = _merge(y, z_act, pa, ya, h, ssm_norm_g[i][None], w_out_ssm[i].astype(BF16),
                   w_o[i].astype(BF16), tm=_pick(m, (256, 128)), tile_gate_b=1)
        h = _ffn(h, norm_ffn_g[i][None], w_ffn_gate[i].astype(BF16), w_ffn_up[i].astype(BF16),
                 w_ffn_down[i].astype(BF16), norm_final_g[None],
                 tm=_pick(m, (1024, 512, 256, 128)), tf=_pick(w_ffn_gate.shape[2], (512, 256, 128)),
                 final_norm=(i == depth - 1))
    return h.reshape(batch, seq, d)
```

```python
import functools

import jax
import jax.numpy as jnp
from jax import lax
from jax.experimental import pallas as pl
from jax.experimental.pallas import tpu as pltpu

EPS = 1e-6
LOG2_E = 1.4426950408889634
SSM_GROUPS = 8
SSM_STATE = 128
SSM_CHUNK = 128
SSD_CHUNKS_PER_STEP = 4

LANES = 128
BF16_SUBLANES = 16
F32_SUBLANES = 8
V7X_VMEM_LIMIT_BYTES = 60000 * 1024

F32 = jnp.float32
BF16 = jnp.bfloat16


def _params(semantics):
    return pltpu.CompilerParams(dimension_semantics=semantics,
                                vmem_limit_bytes=V7X_VMEM_LIMIT_BYTES)


def _resident(shape):
    return pl.BlockSpec(shape, lambda *_: (0,) * len(shape), pipeline_mode=pl.Buffered(1))


def _rmsnorm(x, g):
    return x * lax.rsqrt(jnp.mean(x * x, axis=-1, keepdims=True) + EPS) * g


def _dot(a, b):
    return jnp.dot(a, b, preferred_element_type=F32)


def _sigmoid(x):
    return 1.0 / (1.0 + jnp.exp2(x * -LOG2_E))


def _store_projection(res, out_ref, epilogue):
    if epilogue == "silu":
        res = res * _sigmoid(res)
    else:
        assert epilogue == "none"
    out_ref[...] = res.astype(BF16)


def _tile_major(m, n_tiles, tm, tn):
    return (jax.ShapeDtypeStruct((n_tiles, m, tn), BF16),
            pl.BlockSpec((None, tm, tn), lambda i, j: (j, i, 0)))


def _norm_proj_kernel(x_ref, g_ref, w_ref, wdt_ref, out_ref, dt_ref, u_ref, *, epilogue):
    @pl.when(pl.program_id(1) == 0)
    def _():
        u = _rmsnorm(x_ref[...], g_ref[...]).astype(BF16)
        u_ref[...] = u
        dt_ref[...] = _dot(u, wdt_ref[...])

    _store_projection(_dot(u_ref[...], w_ref[...]), out_ref, epilogue)


def _norm_proj(x2, g, w_all, w_dt, *, col0, n_tiles, tm, tn, epilogue):
    m, d = x2.shape
    out_shape, out_spec = _tile_major(m, n_tiles, tm, tn)
    return pl.pallas_call(
        functools.partial(_norm_proj_kernel, epilogue=epilogue),
        grid=(m // tm, n_tiles),
        in_specs=[
            pl.BlockSpec((tm, d), lambda i, j: (i, 0)),
            pl.BlockSpec((1, d), lambda i, j: (0, 0)),
            pl.BlockSpec((d, tn), lambda i, j: (0, col0 + j)),
            pl.BlockSpec((d, LANES), lambda i, j: (0, 0)),
        ],
        out_specs=[
            out_spec,
            pl.BlockSpec((tm, LANES), lambda i, j: (i, 0)),
            pl.BlockSpec((tm, d), lambda i, j: (i, 0)),
        ],
        out_shape=[out_shape,
                   jax.ShapeDtypeStruct((m, LANES), F32),
                   jax.ShapeDtypeStruct((m, d), BF16)],
        compiler_params=_params(("arbitrary", "arbitrary")),
        name="norm_proj_" + epilogue,
    )(x2, g, w_all, w_dt)


def _proj_kernel(u_ref, w_ref, out_ref, *, epilogue):
    _store_projection(_dot(u_ref[...], w_ref[...]), out_ref, epilogue)


def _proj(u, w_all, *, col0, n_tiles, tm, tn, epilogue):
    m, d = u.shape
    out_shape, out_spec = _tile_major(m, n_tiles, tm, tn)
    return pl.pallas_call(
        functools.partial(_proj_kernel, epilogue=epilogue),
        grid=(m // tm, n_tiles),
        in_specs=[pl.BlockSpec((tm, d), lambda i, j: (i, 0)),
                  pl.BlockSpec((d, tn), lambda i, j: (0, col0 + j))],
        out_specs=out_spec,
        out_shape=out_shape,
        compiler_params=_params(("arbitrary", "arbitrary")),
        name="proj_" + epilogue,
    )(u, w_all)


def _proj_conv_kernel(u_ref, w_ref, cw_ref, cb_ref, out_ref, buf_ref, carry_ref, *, tiles_per_seq):
    tm = u_ref.shape[0]
    taps = cw_ref.shape[0]
    tail = F32_SUBLANES
    j = pl.program_id(1)
    seq_start = pl.program_id(0) % tiles_per_seq == 0
    res = _dot(u_ref[...], w_ref[...])
    for c in range(buf_ref.shape[0]):
        lanes = slice(c * LANES, (c + 1) * LANES)
        res_c = res[:, lanes]
        buf_ref[c, 0:tail, :] = jnp.where(seq_start, 0.0, carry_ref[j, c])
        buf_ref[c, tail:, :] = res_c
        acc = cb_ref[:, lanes] + cw_ref[taps - 1:taps, lanes] * res_c
        for k in range(taps - 1):
            lo = tail - (taps - 1 - k)
            acc = acc + cw_ref[k:k + 1, lanes] * buf_ref[c, lo:lo + tm, :]
        carry_ref[j, c] = buf_ref[c, tm:tm + tail, :]
        out_ref[:, lanes] = (acc * _sigmoid(acc)).astype(BF16)


def _proj_conv(u, w_all, conv_w, conv_b, *, col0, n_tiles, seq, tm, tn):
    m, d = u.shape
    taps = conv_w.shape[0]
    out_shape, out_spec = _tile_major(m, n_tiles, tm, tn)
    return pl.pallas_call(
        functools.partial(_proj_conv_kernel, tiles_per_seq=seq // tm),
        grid=(m // tm, n_tiles),
        in_specs=[pl.BlockSpec((tm, d), lambda i, j: (i, 0)),
                  pl.BlockSpec((d, tn), lambda i, j: (0, col0 + j)),
                  pl.BlockSpec((taps, tn), lambda i, j: (0, j)),
                  pl.BlockSpec((1, tn), lambda i, j: (0, j))],
        out_specs=out_spec,
        out_shape=out_shape,
        scratch_shapes=[pltpu.VMEM((tn // LANES, F32_SUBLANES + tm, LANES), F32),
                        pltpu.VMEM((n_tiles, tn // LANES, F32_SUBLANES, LANES), F32)],
        compiler_params=_params(("arbitrary", "arbitrary")),
        name="proj_conv",
    )(u, w_all, conv_w, conv_b)


def _mixer_a_kernel(ga_ref, b_ref, c_ref, h_ref, cw_ref, wo_ref, out_ref,
                    buf_ref, carry_ref, y_ref, *, tiles_per_seq):
    tm = c_ref.shape[0]
    taps = cw_ref.shape[0]
    tail = F32_SUBLANES
    seq_start = pl.program_id(0) % tiles_per_seq == 0
    for s in range(buf_ref.shape[0]):
        lanes = slice(s * LANES, (s + 1) * LANES)
        v = c_ref[:, lanes].astype(F32) * h_ref[:, lanes].astype(F32)
        buf_ref[s, 0:tail, :] = jnp.where(seq_start, 0.0, carry_ref[s])
        buf_ref[s, tail:, :] = v
        acc = cw_ref[taps - 1:taps, lanes] * v
        for j in range(taps - 1):
            lo = tail - (taps - 1 - j)
            acc = acc + cw_ref[j:j + 1, lanes] * buf_ref[s, lo:lo + tm, :]
        carry_ref[s] = buf_ref[s, tm:tm + tail, :]
        y_ref[:, lanes] = (b_ref[:, lanes].astype(F32) * acc).astype(BF16)
    out_ref[...] = (_sigmoid(ga_ref[...].astype(F32)) * _dot(y_ref[...], wo_ref[...])).astype(BF16)


def _mixer_a(pa, conv_w, w_out, *, seq, tm, tile_gate, tile_b):
    m = pa.shape[1]
    c = conv_w.shape[1]
    d = w_out.shape[1]

    def col(k):
        return pl.BlockSpec((None, tm, c), lambda i: (k, i, 0))

    return pl.pallas_call(
        functools.partial(_mixer_a_kernel, tiles_per_seq=seq // tm),
        grid=(m // tm,),
        in_specs=[col(tile_gate), col(tile_b), col(tile_b + 1), col(tile_b + 2),
                  _resident(conv_w.shape), _resident(w_out.shape)],
        out_specs=pl.BlockSpec((tm, d), lambda i: (i, 0)),
        out_shape=jax.ShapeDtypeStruct((m, d), BF16),
        scratch_shapes=[pltpu.VMEM((c // LANES, F32_SUBLANES + tm, LANES), F32),
                        pltpu.VMEM((c // LANES, F32_SUBLANES, LANES), F32),
                        pltpu.VMEM((tm, c), BF16)],
        compiler_params=_params(("arbitrary",)),
        name="mixer_a",
    )(pa, pa, pa, pa, conv_w, w_out)


def _ssd_prep_kernel(dt_ref, dtb_ref, alog_ref, cs_ref, rt_ref, wt_ref):
    q = SSM_CHUNK
    row = lax.broadcasted_iota(jnp.int32, (q, q), 0)
    col = lax.broadcasted_iota(jnp.int32, (q, q), 1)
    tril = (col <= row).astype(F32)
    a = -jnp.exp(alog_ref[...])
    for k in range(rt_ref.shape[0]):
        rows = slice(k * q, (k + 1) * q)
        dt = jax.nn.softplus(dt_ref[rows, :] + dtb_ref[...])
        cs = LOG2_E * jnp.dot(tril, dt * a, preferred_element_type=F32,
                              precision=lax.Precision.HIGHEST)
        r = cs - LOG2_E * jnp.log(dt)
        cs_ref[rows, :] = cs
        rt_ref[k] = r.T
        wt_ref[k] = jnp.exp2(cs[q - 1:q, :] - r).T


def _ssd_prep(dt_raw, dt_bias, a_log, *, tr):
    m = dt_raw.shape[0]
    q = SSM_CHUNK
    per_chunk = jax.ShapeDtypeStruct((m // q, LANES, q), F32)
    return pl.pallas_call(
        _ssd_prep_kernel,
        grid=(m // tr,),
        in_specs=[pl.BlockSpec((tr, LANES), lambda i: (i, 0)),
                  pl.BlockSpec(dt_bias.shape, lambda i: (0, 0)),
                  pl.BlockSpec(a_log.shape, lambda i: (0, 0))],
        out_specs=[pl.BlockSpec((tr, LANES), lambda i: (i, 0)),
                   pl.BlockSpec((tr // q, LANES, q), lambda i: (i, 0, 0)),
                   pl.BlockSpec((tr // q, LANES, q), lambda i: (i, 0, 0))],
        out_shape=[jax.ShapeDtypeStruct((m, LANES), F32), per_chunk, per_chunk],
        compiler_params=_params(("arbitrary",)),
        name="ssd_prep",
    )(dt_raw, dt_bias, a_log)


def _ssd_kernel(x0_ref, x1_ref, bc_ref, cs_ref, rt_ref, wt_ref, dsk_ref, y_ref, st_ref):
    q = SSM_CHUNK
    hp = LANES // 2

    @pl.when(pl.program_id(1) == 0)
    def _():
        st_ref[...] = jnp.zeros(st_ref.shape, F32)

    row = lax.broadcasted_iota(jnp.int32, (q, q), 0)
    col = lax.broadcasted_iota(jnp.int32, (q, q), 1)
    causal = col <= row
    left = lax.broadcasted_iota(jnp.int32, (q, LANES), 1) < hp
    for k in range(rt_ref.shape[0]):
        _ssd_chunk(k, x0_ref, x1_ref, bc_ref, cs_ref, rt_ref, wt_ref, dsk_ref, y_ref, st_ref,
                   causal, left)


def _ssd_chunk(k, x0_ref, x1_ref, bc_ref, cs_ref, rt_ref, wt_ref, dsk_ref, y_ref, st_ref, causal, left):
    q = SSM_CHUNK
    groups = SSM_GROUPS
    pairs_g = st_ref.shape[0] // groups
    half = x0_ref.shape[1]
    rows = slice(k * q, (k + 1) * q)
    for g in range(groups):
        bg = bc_ref[rows, g * LANES:(g + 1) * LANES]
        cg = bc_ref[rows, (groups + g) * LANES:(groups + g + 1) * LANES]
        cbm = lax.dot_general(cg, bg, (((1,), (1,)), ((), ())),
                              preferred_element_type=F32).astype(BF16)
        bt = bg.T.reshape(SSM_STATE // BF16_SUBLANES, BF16_SUBLANES, q)
        st_g = jnp.concatenate([st_ref[g * pairs_g + p] for p in range(pairs_g)], axis=1)
        y_off = _dot(cg, st_g.astype(BF16))
        for p in range(pairs_g):
            slab = g * pairs_g + p
            lanes = slice(slab * LANES, (slab + 1) * LANES)
            lhs_y, lhs_s, a = [], [], []
            for h in (2 * slab, 2 * slab + 1):
                a_h = cs_ref[rows, h:h + 1]
                a.append(a_h)
                seg = a_h - rt_ref[k, h:h + 1, :]
                decay = jnp.exp2(jnp.where(causal, seg, -jnp.inf))
                lhs_y.append(cbm * decay.astype(BF16))
                w_h = jnp.broadcast_to(wt_ref[k, h:h + 1, :], (BF16_SUBLANES, q)).astype(BF16)
                lhs_s.append((bt * w_h[None]).reshape(SSM_STATE, q))
            lhs = jnp.concatenate([jnp.concatenate(lhs_y, axis=1),
                                   jnp.concatenate(lhs_s, axis=1)], axis=0)
            src = x0_ref if slab * LANES < half else x1_ref
            off = (slab * LANES) % half
            xs = src[rows, off:off + LANES].astype(F32)
            rhs = jnp.concatenate([jnp.where(left, xs, 0.0).astype(BF16),
                                   jnp.where(left, 0.0, xs).astype(BF16)], axis=0)
            res = _dot(lhs, rhs)
            e12 = jnp.exp2(jnp.where(left, a[0], a[1]))
            st_ref[slab] = st_ref[slab] * e12[q - 1:q, :] + res[q:, :]
            y = res[:q, :] + y_off[:, p * LANES:(p + 1) * LANES] * e12 + dsk_ref[:, lanes] * xs
            y_ref[rows, lanes] = y.astype(BF16)


def _ssd(xbc, cs, rt, wt, d_skip_e, *, batch, seq, chunks_per_step):
    n_tiles, m, tn = xbc.shape
    assert n_tiles == 3 and tn == 2 * SSM_GROUPS * SSM_STATE
    d_inner = 2 * tn
    q = SSM_CHUNK
    tr = chunks_per_step * q
    ns = seq // tr

    def tile(k):
        return pl.BlockSpec((None, tr, tn), lambda b, c: (k, b * ns + c, 0))

    def rows(width):
        return pl.BlockSpec((tr, width), lambda b, c: (b * ns + c, 0))

    per_chunk = pl.BlockSpec((chunks_per_step, LANES, q), lambda b, c: (b * ns + c, 0, 0))
    return pl.pallas_call(
        _ssd_kernel,
        grid=(batch, ns),
        in_specs=[tile(0), tile(1), tile(2), rows(LANES), per_chunk, per_chunk,
                  pl.BlockSpec(d_skip_e.shape, lambda b, c: (0, 0))],
        out_specs=rows(d_inner),
        out_shape=jax.ShapeDtypeStruct((m, d_inner), BF16),
        scratch_shapes=[pltpu.VMEM((d_inner // LANES, SSM_STATE, LANES), F32)],
        compiler_params=_params(("arbitrary", "arbitrary")),
        name="ssd",
    )(xbc, xbc, xbc, cs, rt, wt, d_skip_e)


def _merge_kernel(y_ref, z0_ref, z1_ref, gb_ref, ya_ref, x_ref, ng_ref, wssm_ref, wo_ref, h_ref,
                  yz_ref):
    gw = y_ref.shape[1] // SSM_GROUPS
    half = z0_ref.shape[1]
    for g in range(SSM_GROUPS):
        cols = slice(g * gw, (g + 1) * gw)
        src = z0_ref if g * gw < half else z1_ref
        off = (g * gw) % half
        yz = y_ref[:, cols].astype(F32) * src[:, off:off + gw].astype(F32)
        yz = yz * lax.rsqrt(jnp.mean(yz * yz, axis=-1, keepdims=True) + EPS)
        yz_ref[:, cols] = (yz * ng_ref[:, cols]).astype(BF16)

    yb = _dot(yz_ref[...], wssm_ref[...])
    mixed = (ya_ref[...].astype(F32) + _sigmoid(gb_ref[...].astype(F32)) * yb).astype(BF16)
    h_ref[...] = x_ref[...] + _dot(mixed, wo_ref[...])


def _merge(y, z_act, pa, ya, x2, norm_g, w_ssm, w_o, *, tm, tile_gate_b):
    m, d = x2.shape
    di = y.shape[1]
    assert z_act.shape == (2, m, di // 2) and pa.shape[2] == d

    def tile(width, k):
        return pl.BlockSpec((None, tm, width), lambda i: (k, i, 0))

    def rows(width):
        return pl.BlockSpec((tm, width), lambda i: (i, 0))

    return pl.pallas_call(
        _merge_kernel,
        grid=(m // tm,),
        in_specs=[rows(di), tile(di // 2, 0), tile(di // 2, 1), tile(d, tile_gate_b), rows(d), rows(d),
                  _resident(norm_g.shape), _resident(w_ssm.shape), _resident(w_o.shape)],
        out_specs=rows(d),
        out_shape=jax.ShapeDtypeStruct((m, d), F32),
        scratch_shapes=[pltpu.VMEM((tm, di), BF16)],
        compiler_params=_params(("arbitrary",)),
        name="merge",
    )(y, z_act, z_act, pa, ya, x2, norm_g, w_ssm, w_o)


def _ffn_kernel(h_ref, g_ref, wg_ref, wu_ref, wd_ref, gf_ref, out_ref, u_ref, *, final_norm):
    f = pl.program_id(1)

    @pl.when(f == 0)
    def _():
        h = h_ref[...]
        u_ref[...] = _rmsnorm(h, g_ref[...]).astype(BF16)
        out_ref[...] = h

    u = u_ref[...]
    gate = _dot(u, wg_ref[...])
    act = (gate * _sigmoid(gate) * _dot(u, wu_ref[...])).astype(BF16)
    out_ref[...] += _dot(act, wd_ref[...])

    if final_norm:
        @pl.when(f == pl.num_programs(1) - 1)
        def _():
            out_ref[...] = _rmsnorm(out_ref[...], gf_ref[...])


def _ffn(h, g_ffn, w_gate, w_up, w_down, g_final, *, tm, tf, final_norm):
    m, d = h.shape
    ff = w_gate.shape[1]
    return pl.pallas_call(
        functools.partial(_ffn_kernel, final_norm=final_norm),
        grid=(m // tm, ff // tf),
        in_specs=[pl.BlockSpec((tm, d), lambda i, f: (i, 0)),
                  pl.BlockSpec((1, d), lambda i, f: (0, 0)),
                  pl.BlockSpec((d, tf), lambda i, f: (0, f)),
                  pl.BlockSpec((d, tf), lambda i, f: (0, f)),
                  pl.BlockSpec((tf, d), lambda i, f: (f, 0)),
                  pl.BlockSpec((1, d), lambda i, f: (0, 0))],
        out_specs=pl.BlockSpec((tm, d), lambda i, f: (i, 0)),
        out_shape=jax.ShapeDtypeStruct((m, d), F32),
        scratch_shapes=[pltpu.VMEM((tm, d), BF16)],
        compiler_params=_params(("arbitrary", "arbitrary")),
        name="ffn",
    )(h, g_ffn, w_gate, w_up, w_down, g_final)


def _pick(n, candidates):
    for c in candidates:
        if n % c == 0:
            return c
    raise ValueError(f"no tile in {candidates} divides {n}")


def kernel(x, norm_mix_g, w_in, conv_a_w, w_out_a, ssm_conv_w, ssm_conv_b, dt_bias, a_log, d_skip,
           ssm_norm_g, w_out_ssm, w_o, norm_ffn_g, w_ffn_gate, w_ffn_up, w_ffn_down, norm_final_g):
    batch, seq, d = x.shape
    m = batch * seq
    depth = w_in.shape[0]
    d_inner = w_out_ssm.shape[1]
    heads = dt_bias.shape[1]
    gn = SSM_GROUPS * SSM_STATE
    conv_dim = d_inner + 2 * gn
    n_main = w_in.shape[2] - heads
    off_a = 2 * d
    off_z = off_a + 3 * conv_a_w.shape[2]
    off_xbc = off_z + d_inner
    hp = d_inner // heads
    tn = d
    assert conv_a_w.shape[2] == d and off_xbc + conv_dim == n_main
    assert d_inner == 2 * tn and 2 * gn == tn
    assert seq % SSM_CHUNK == 0 and heads <= LANES and heads % (2 * SSM_GROUPS) == 0
    assert 2 * hp == LANES and SSM_STATE == LANES and SSM_CHUNK == LANES
    pad_h = ((0, 0), (0, LANES - heads))
    tm_proj = _pick(seq, (1024, 512, 256, 128))

    h = x.reshape(m, d)
    for i in range(depth):
        w_all = w_in[i].astype(BF16)
        w_dt = jnp.pad(w_in[i, :, n_main:], pad_h).astype(BF16)
        dtb = jnp.pad(dt_bias[i][None], pad_h)
        alog = jnp.pad(a_log[i][None], pad_h)
        d_skip_e = jnp.repeat(d_skip[i], hp)[None]

        pa, dt_raw, u = _norm_proj(h, norm_mix_g[i][None], w_all, w_dt, col0=0, n_tiles=off_z // tn,
                                   tm=tm_proj, tn=tn, epilogue="none")
        z_act = _proj(u, w_all, col0=off_z // tn, n_tiles=d_inner // tn,
                      tm=tm_proj, tn=tn, epilogue="silu")
        xbc = _proj_conv(u, w_all, ssm_conv_w[i], ssm_conv_b[i][None], col0=off_xbc // tn,
                         n_tiles=conv_dim // tn, seq=seq, tm=tm_proj, tn=tn)

        ya = _mixer_a(pa, conv_a_w[i], w_out_a[i].astype(BF16), seq=seq,
                      tm=_pick(seq, (512, 256, 128)), tile_gate=0, tile_b=off_a // tn)
        cs, rt, wt = _ssd_prep(dt_raw, dtb, alog, tr=tm_proj)
        y = _ssd(xbc, cs, rt, wt, d_skip_e, batch=batch, seq=seq,
                 chunks_per_step=_pick(seq // SSM_CHUNK, (SSD_CHUNKS_PER_STEP, 2, 1)))
        h = _merge(y, z_act, pa, ya, h, ssm_norm_g[i][None], w_out_ssm[i].astype(BF16),
                   w_o[i].astype(BF16), tm=_pick(m, (256, 128)), tile_gate_b=1)
        h = _ffn(h, norm_ffn_g[i][None], w_ffn_gate[i].astype(BF16), w_ffn_up[i].astype(BF16),
                 w_ffn_down[i].astype(BF16), norm_final_g[None],
                 tm=_pick(m, (1024, 512, 256, 128)), tf=_pick(w_ffn_gate.shape[2], (512, 256, 128)),
                 final_norm=(i == depth - 1))
    return h.reshape(batch, seq, d)
```

```python
import functools

import jax
import jax.numpy as jnp
from jax import lax
from jax.experimental import pallas as pl
from jax.experimental.pallas import tpu as pltpu

EPS = 1e-6
LOG2_E = 1.4426950408889634
SSM_GROUPS = 8
SSM_STATE = 128
SSM_CHUNK = 128
SSD_CHUNKS_PER_STEP = 4

LANES = 128
BF16_SUBLANES = 16
F32_SUBLANES = 8
V7X_VMEM_LIMIT_BYTES = 60000 * 1024

F32 = jnp.float32
BF16 = jnp.bfloat16


def _params(semantics):
    return pltpu.CompilerParams(dimension_semantics=semantics,
                                vmem_limit_bytes=V7X_VMEM_LIMIT_BYTES)


def _resident(shape):
    return pl.BlockSpec(shape, lambda *_: (0,) * len(shape), pipeline_mode=pl.Buffered(1))


def _rmsnorm(x, g):
    return x * lax.rsqrt(jnp.mean(x * x, axis=-1, keepdims=True) + EPS) * g


def _dot(a, b):
    return jnp.dot(a, b, preferred_element_type=F32)


def _sigmoid(x):
    return 1.0 / (1.0 + jnp.exp2(x * -LOG2_E))


def _store_projection(res, out_ref, epilogue):
    if epilogue == "silu":
        res = res * _sigmoid(res)
    else:
        assert epilogue == "none"
    out_ref[...] = res.astype(BF16)


def _tile_major(m, n_tiles, tm, tn):
    return (jax.ShapeDtypeStruct((n_tiles, m, tn), BF16),
            pl.BlockSpec((None, tm, tn), lambda i, j: (j, i, 0)))


def _norm_proj_kernel(x_ref, g_ref, w_ref, wdt_ref, out_ref, dt_ref, u_ref, *, epilogue):
    @pl.when(pl.program_id(1) == 0)
    def _():
        u = _rmsnorm(x_ref[...], g_ref[...]).astype(BF16)
        u_ref[...] = u
        dt_ref[...] = _dot(u, wdt_ref[...])

    _store_projection(_dot(u_ref[...], w_ref[...]), out_ref, epilogue)


def _norm_proj(x2, g, w_all, w_dt, *, col0, n_tiles, tm, tn, epilogue):
    m, d = x2.shape
    out_shape, out_spec = _tile_major(m, n_tiles, tm, tn)
    return pl.pallas_call(
        functools.partial(_norm_proj_kernel, epilogue=epilogue),
        grid=(m // tm, n_tiles),
        in_specs=[
            pl.BlockSpec((tm, d), lambda i, j: (i, 0)),
            pl.BlockSpec((1, d), lambda i, j: (0, 0)),
            pl.BlockSpec((d, tn), lambda i, j: (0, col0 + j)),
            pl.BlockSpec((d, LANES), lambda i, j: (0, 0)),
        ],
        out_specs=[
            out_spec,
            pl.BlockSpec((tm, LANES), lambda i, j: (i, 0)),
            pl.BlockSpec((tm, d), lambda i, j: (i, 0)),
        ],
        out_shape=[out_shape,
                   jax.ShapeDtypeStruct((m, LANES), F32),
                   jax.ShapeDtypeStruct((m, d), BF16)],
        compiler_params=_params(("arbitrary", "arbitrary")),
        name="norm_proj_" + epilogue,
    )(x2, g, w_all, w_dt)


def _proj_kernel(u_ref, w_ref, out_ref, *, epilogue):
    _store_projection(_dot(u_ref[...], w_ref[...]), out_ref, epilogue)


def _proj(u, w_all, *, col0, n_tiles, tm, tn, epilogue):
    m, d = u.shape
    out_shape, out_spec = _tile_major(m, n_tiles, tm, tn)
    return pl.pallas_call(
        functools.partial(_proj_kernel, epilogue=epilogue),
        grid=(m // tm, n_tiles),
        in_specs=[pl.BlockSpec((tm, d), lambda i, j: (i, 0)),
                  pl.BlockSpec((d, tn), lambda i, j: (0, col0 + j))],
        out_specs=out_spec,
        out_shape=out_shape,
        compiler_params=_params(("arbitrary", "arbitrary")),
        name="proj_" + epilogue,
    )(u, w_all)


def _proj_conv_kernel(u_ref, w_ref, cw_ref, cb_ref, out_ref, buf_ref, carry_ref, *, tiles_per_seq):
    tm = u_ref.shape[0]
    taps = cw_ref.shape[0]
    tail = F32_SUBLANES
    j = pl.program_id(1)
    seq_start = pl.program_id(0) % tiles_per_seq == 0
    res = _dot(u_ref[...], w_ref[...])
    for c in range(buf_ref.shape[0]):
        lanes = slice(c * LANES, (c + 1) * LANES)
        res_c = res[:, lanes]
        buf_ref[c, 0:tail, :] = jnp.where(seq_start, 0.0, carry_ref[j, c])
        buf_ref[c, tail:, :] = res_c
        acc = cb_ref[:, lanes] + cw_ref[taps - 1:taps, lanes] * res_c
        for k in range(taps - 1):
            lo = tail - (taps - 1 - k)
            acc = acc + cw_ref[k:k + 1, lanes] * buf_ref[c, lo:lo + tm, :]
        carry_ref[j, c] = buf_ref[c, tm:tm + tail, :]
        out_ref[:, lanes] = (acc * _sigmoid(acc)).astype(BF16)


def _proj_conv(u, w_all, conv_w, conv_b, *, col0, n_tiles, seq, tm, tn):
    m, d = u.shape
    taps = conv_w.shape[0]
    out_shape, out_spec = _tile_major(m, n_tiles, tm, tn)
    return pl.pallas_call(
        functools.partial(_proj_conv_kernel, tiles_per_seq=seq // tm),
        grid=(m // tm, n_tiles),
        in_specs=[pl.BlockSpec((tm, d), lambda i, j: (i, 0)),
                  pl.BlockSpec((d, tn), lambda i, j: (0, col0 + j)),
                  pl.BlockSpec((taps, tn), lambda i, j: (0, j)),
                  pl.BlockSpec((1, tn), lambda i, j: (0, j))],
        out_specs=out_spec,
        out_shape=out_shape,
        scratch_shapes=[pltpu.VMEM((tn // LANES, F32_SUBLANES + tm, LANES), F32),
                        pltpu.VMEM((n_tiles, tn // LANES, F32_SUBLANES, LANES), F32)],
        compiler_params=_params(("arbitrary", "arbitrary")),
        name="proj_conv",
    )(u, w_all, conv_w, conv_b)


def _mixer_a_kernel(ga_ref, b_ref, c_ref, h_ref, cw_ref, wo_ref, out_ref,
                    buf_ref, carry_ref, y_ref, *, tiles_per_seq):
    tm = c_ref.shape[0]
    taps = cw_ref.shape[0]
    tail = F32_SUBLANES
    seq_start = pl.program_id(0) % tiles_per_seq == 0
    for s in range(buf_ref.shape[0]):
        lanes = slice(s * LANES, (s + 1) * LANES)
        v = c_ref[:, lanes].astype(F32) * h_ref[:, lanes].astype(F32)
        buf_ref[s, 0:tail, :] = jnp.where(seq_start, 0.0, carry_ref[s])
        buf_ref[s, tail:, :] = v
        acc = cw_ref[taps - 1:taps, lanes] * v
        for j in range(taps - 1):
            lo = tail - (taps - 1 - j)
            acc = acc + cw_ref[j:j + 1, lanes] * buf_ref[s, lo:lo + tm, :]
        carry_ref[s] = buf_ref[s, tm:tm + tail, :]
        y_ref[:, lanes] = (b_ref[:, lanes].astype(F32) * acc).astype(BF16)
    out_ref[...] = (_sigmoid(ga_ref[...].astype(F32)) * _dot(y_ref[...], wo_ref[...])).astype(BF16)


def _mixer_a(pa, conv_w, w_out, *, seq, tm, tile_gate, tile_b):
    m = pa.shape[1]
    c = conv_w.shape[1]
    d = w_out.shape[1]

    def col(k):
        return pl.BlockSpec((None, tm, c), lambda i: (k, i, 0))

    return pl.pallas_call(
        functools.partial(_mixer_a_kernel, tiles_per_seq=seq // tm),
        grid=(m // tm,),
        in_specs=[col(tile_gate), col(tile_b), col(tile_b + 1), col(tile_b + 2),
                  _resident(conv_w.shape), _resident(w_out.shape)],
        out_specs=pl.BlockSpec((tm, d), lambda i: (i, 0)),
        out_shape=jax.ShapeDtypeStruct((m, d), BF16),
        scratch_shapes=[pltpu.VMEM((c // LANES, F32_SUBLANES + tm, LANES), F32),
                        pltpu.VMEM((c // LANES, F32_SUBLANES, LANES), F32),
                        pltpu.VMEM((tm, c), BF16)],
        compiler_params=_params(("arbitrary",)),
        name="mixer_a",
    )(pa, pa, pa, pa, conv_w, w_out)


def _ssd_prep_kernel(dt_ref, dtb_ref, alog_ref, cs_ref, rt_ref, wt_ref):
    q = SSM_CHUNK
    row = lax.broadcasted_iota(jnp.int32, (q, q), 0)
    col = lax.broadcasted_iota(jnp.int32, (q, q), 1)
    tril = (col <= row).astype(F32)
    a = -jnp.exp(alog_ref[...])
    for k in range(rt_ref.shape[0]):
        rows = slice(k * q, (k + 1) * q)
        dt = jax.nn.softplus(dt_ref[rows, :] + dtb_ref[...])
        cs = LOG2_E * jnp.dot(tril, dt * a, preferred_element_type=F32,
                              precision=lax.Precision.HIGHEST)
        r = cs - LOG2_E * jnp.log(dt)
        cs_ref[rows, :] = cs
        rt_ref[k] = r.T
        wt_ref[k] = jnp.exp2(cs[q - 1:q, :] - r).T


def _ssd_prep(dt_raw, dt_bias, a_log, *, tr):
    m = dt_raw.shape[0]
    q = SSM_CHUNK
    per_chunk = jax.ShapeDtypeStruct((m // q, LANES, q), F32)
    return pl.pallas_call(
        _ssd_prep_kernel,
        grid=(m // tr,),
        in_specs=[pl.BlockSpec((tr, LANES), lambda i: (i, 0)),
                  pl.BlockSpec(dt_bias.shape, lambda i: (0, 0)),
                  pl.BlockSpec(a_log.shape, lambda i: (0, 0))],
        out_specs=[pl.BlockSpec((tr, LANES), lambda i: (i, 0)),
                   pl.BlockSpec((tr // q, LANES, q), lambda i: (i, 0, 0)),
                   pl.BlockSpec((tr // q, LANES, q), lambda i: (i, 0, 0))],
        out_shape=[jax.ShapeDtypeStruct((m, LANES), F32), per_chunk, per_chunk],
        compiler_params=_params(("arbitrary",)),
        name="ssd_prep",
    )(dt_raw, dt_bias, a_log)


def _ssd_kernel(x0_ref, x1_ref, bc_ref, cs_ref, rt_ref, wt_ref, dsk_ref, y_ref, st_ref):
    q = SSM_CHUNK
    hp = LANES // 2

    @pl.when(pl.program_id(1) == 0)
    def _():
        st_ref[...] = jnp.zeros(st_ref.shape, F32)

    row = lax.broadcasted_iota(jnp.int32, (q, q), 0)
    col = lax.broadcasted_iota(jnp.int32, (q, q), 1)
    causal = col <= row
    left = lax.broadcasted_iota(jnp.int32, (q, LANES), 1) < hp
    for k in range(rt_ref.shape[0]):
        _ssd_chunk(k, x0_ref, x1_ref, bc_ref, cs_ref, rt_ref, wt_ref, dsk_ref, y_ref, st_ref,
                   causal, left)


def _ssd_chunk(k, x0_ref, x1_ref, bc_ref, cs_ref, rt_ref, wt_ref, dsk_ref, y_ref, st_ref, causal, left):
    q = SSM_CHUNK
    groups = SSM_GROUPS
    pairs_g = st_ref.shape[0] // groups
    half = x0_ref.shape[1]
    rows = slice(k * q, (k + 1) * q)
    for g in range(groups):
        bg = bc_ref[rows, g * LANES:(g + 1) * LANES]
        cg = bc_ref[rows, (groups + g) * LANES:(groups + g + 1) * LANES]
        cbm = lax.dot_general(cg, bg, (((1,), (1,)), ((), ())),
                              preferred_element_type=F32).astype(BF16)
        bt = bg.T.reshape(SSM_STATE // BF16_SUBLANES, BF16_SUBLANES, q)
        st_g = jnp.concatenate([st_ref[g * pairs_g + p] for p in range(pairs_g)], axis=1)
        y_off = _dot(cg, st_g.astype(BF16))
        for p in range(pairs_g):
            slab = g * pairs_g + p
            lanes = slice(slab * LANES, (slab + 1) * LANES)
            lhs_y, lhs_s, a = [], [], []
            for h in (2 * slab, 2 * slab + 1):
                a_h = cs_ref[rows, h:h + 1]
                a.append(a_h)
                seg = a_h - rt_ref[k, h:h + 1, :]
                decay = jnp.exp2(jnp.where(causal, seg, -jnp.inf))
                lhs_y.append(cbm * decay.astype(BF16))
                w_h = jnp.broadcast_to(wt_ref[k, h:h + 1, :], (BF16_SUBLANES, q)).astype(BF16)
                lhs_s.append((bt * w_h[None]).reshape(SSM_STATE, q))
            lhs = jnp.concatenate([jnp.concatenate(lhs_y, axis=1),
                                   jnp.concatenate(lhs_s, axis=1)], axis=0)
            src = x0_ref if slab * LANES < half else x1_ref
            off = (slab * LANES) % half
            xs = src[rows, off:off + LANES].astype(F32)
            rhs = jnp.concatenate([jnp.where(left, xs, 0.0).astype(BF16),
                                   jnp.where(left, 0.0, xs).astype(BF16)], axis=0)
            res = _dot(lhs, rhs)
            e12 = jnp.exp2(jnp.where(left, a[0], a[1]))
            st_ref[slab] = st_ref[slab] * e12[q - 1:q, :] + res[q:, :]
            y = res[:q, :] + y_off[:, p * LANES:(p + 1) * LANES] * e12 + dsk_ref[:, lanes] * xs
            y_ref[rows, lanes] = y.astype(BF16)


def _ssd(xbc, cs, rt, wt, d_skip_e, *, batch, seq, chunks_per_step):
    n_tiles, m, tn = xbc.shape
    assert n_tiles == 3 and tn == 2 * SSM_GROUPS * SSM_STATE
    d_inner = 2 * tn
    q = SSM_CHUNK
    tr = chunks_per_step * q
    ns = seq // tr

    def tile(k):
        return pl.BlockSpec((None, tr, tn), lambda b, c: (k, b * ns + c, 0))

    def rows(width):
        return pl.BlockSpec((tr, width), lambda b, c: (b * ns + c, 0))

    per_chunk = pl.BlockSpec((chunks_per_step, LANES, q), lambda b, c: (b * ns + c, 0, 0))
    return pl.pallas_call(
        _ssd_kernel,
        grid=(batch, ns),
        in_specs=[tile(0), tile(1), tile(2), rows(LANES), per_chunk, per_chunk,
                  pl.BlockSpec(d_skip_e.shape, lambda b, c: (0, 0))],
        out_specs=rows(d_inner),
        out_shape=jax.ShapeDtypeStruct((m, d_inner), BF16),
        scratch_shapes=[pltpu.VMEM((d_inner // LANES, SSM_STATE, LANES), F32)],
        compiler_params=_params(("arbitrary", "arbitrary")),
        name="ssd",
    )(xbc, xbc, xbc, cs, rt, wt, d_skip_e)


def _mix_kernel(y_ref, z0_ref, z1_ref, gb_ref, ya_ref, ng_ref, wssm_ref, out_ref, yz_ref):
    gw = y_ref.shape[1] // SSM_GROUPS
    half = z0_ref.shape[1]
    for g in range(SSM_GROUPS):
        cols = slice(g * gw, (g + 1) * gw)
        src = z0_ref if g * gw < half else z1_ref
        off = (g * gw) % half
        yz = y_ref[:, cols].astype(F32) * src[:, off:off + gw].astype(F32)
        yz = yz * lax.rsqrt(jnp.mean(yz * yz, axis=-1, keepdims=True) + EPS)
        yz_ref[:, cols] = (yz * ng_ref[:, cols]).astype(BF16)

    yb = _dot(yz_ref[...], wssm_ref[...])
    out_ref[...] = (ya_ref[...].astype(F32) + _sigmoid(gb_ref[...].astype(F32)) * yb).astype(BF16)


def _mix(y, z_act, pa, ya, norm_g, w_ssm, *, tm, tile_gate_b):
    m, di = y.shape
    d = w_ssm.shape[1]
    assert z_act.shape == (2, m, di // 2) and pa.shape[2] == d

    def tile(width, k):
        return pl.BlockSpec((None, tm, width), lambda i: (k, i, 0))

    def rows(width):
        return pl.BlockSpec((tm, width), lambda i: (i, 0))

    return pl.pallas_call(
        _mix_kernel,
        grid=(m // tm,),
        in_specs=[rows(di), tile(di // 2, 0), tile(di // 2, 1), tile(d, tile_gate_b), rows(d),
                  _resident(norm_g.shape), _resident(w_ssm.shape)],
        out_specs=rows(d),
        out_shape=jax.ShapeDtypeStruct((m, d), BF16),
        scratch_shapes=[pltpu.VMEM((tm, di), BF16)],
        compiler_params=_params(("arbitrary",)),
        name="mix",
    )(y, z_act, z_act, pa, ya, norm_g, w_ssm)


def _out_proj_kernel(mixed_ref, x_ref, wo_ref, h_ref):
    h_ref[...] = x_ref[...] + _dot(mixed_ref[...], wo_ref[...])


def _out_proj(mixed, x2, w_o, *, tm):
    m, d = x2.shape

    def rows():
        return pl.BlockSpec((tm, d), lambda i: (i, 0))

    return pl.pallas_call(
        _out_proj_kernel,
        grid=(m // tm,),
        in_specs=[rows(), rows(), _resident(w_o.shape)],
        out_specs=rows(),
        out_shape=jax.ShapeDtypeStruct((m, d), F32),
        compiler_params=_params(("arbitrary",)),
        name="out_proj",
    )(mixed, x2, w_o)


def _ffn_kernel(h_ref, g_ref, wg_ref, wu_ref, wd_ref, gf_ref, out_ref, u_ref, *, final_norm):
    f = pl.program_id(1)

    @pl.when(f == 0)
    def _():
        h = h_ref[...]
        u_ref[...] = _rmsnorm(h, g_ref[...]).astype(BF16)
        out_ref[...] = h

    u = u_ref[...]
    tf = wg_ref.shape[1]
    part = None
    for lo in range(0, tf, tf // 2):
        cols = slice(lo, lo + tf // 2)
        gate = _dot(u, wg_ref[:, cols])
        act = (gate * _sigmoid(gate) * _dot(u, wu_ref[:, cols])).astype(BF16)
        p = _dot(act, wd_ref[cols, :])
        part = p if part is None else part + p
    out_ref[...] += part

    if final_norm:
        @pl.when(f == pl.num_programs(1) - 1)
        def _():
            out_ref[...] = _rmsnorm(out_ref[...], gf_ref[...])


def _ffn(h, g_ffn, w_gate, w_up, w_down, g_final, *, tm, tf, final_norm):
    m, d = h.shape
    ff = w_gate.shape[1]
    return pl.pallas_call(
        functools.partial(_ffn_kernel, final_norm=final_norm),
        grid=(m // tm, ff // tf),
        in_specs=[pl.BlockSpec((tm, d), lambda i, f: (i, 0)),
                  pl.BlockSpec((1, d), lambda i, f: (0, 0)),
                  pl.BlockSpec((d, tf), lambda i, f: (0, f)),
                  pl.BlockSpec((d, tf), lambda i, f: (0, f)),
                  pl.BlockSpec((tf, d), lambda i, f: (f, 0)),
                  pl.BlockSpec((1, d), lambda i, f: (0, 0))],
        out_specs=pl.BlockSpec((tm, d), lambda i, f: (i, 0)),
        out_shape=jax.ShapeDtypeStruct((m, d), F32),
        scratch_shapes=[pltpu.VMEM((tm, d), BF16)],
        compiler_params=_params(("arbitrary", "arbitrary")),
        name="ffn",
    )(h, g_ffn, w_gate, w_up, w_down, g_final)


def _pick(n, candidates):
    for c in candidates:
        if n % c == 0:
            return c
    raise ValueError(f"no tile in {candidates} divides {n}")


def kernel(x, norm_mix_g, w_in, conv_a_w, w_out_a, ssm_conv_w, ssm_conv_b, dt_bias, a_log, d_skip,
           ssm_norm_g, w_out_ssm, w_o, norm_ffn_g, w_ffn_gate, w_ffn_up, w_ffn_down, norm_final_g):
    batch, seq, d = x.shape
    m = batch * seq
    depth = w_in.shape[0]
    d_inner = w_out_ssm.shape[1]
    heads = dt_bias.shape[1]
    gn = SSM_GROUPS * SSM_STATE
    conv_dim = d_inner + 2 * gn
    n_main = w_in.shape[2] - heads
    off_a = 2 * d
    off_z = off_a + 3 * conv_a_w.shape[2]
    off_xbc = off_z + d_inner
    hp = d_inner // heads
    tn = d
    assert conv_a_w.shape[2] == d and off_xbc + conv_dim == n_main
    assert d_inner == 2 * tn and 2 * gn == tn
    assert seq % SSM_CHUNK == 0 and heads <= LANES and heads % (2 * SSM_GROUPS) == 0
    assert 2 * hp == LANES and SSM_STATE == LANES and SSM_CHUNK == LANES
    pad_h = ((0, 0), (0, LANES - heads))
    tm_proj = _pick(seq, (1024, 512, 256, 128))

    h = x.reshape(m, d)
    for i in range(depth):
        w_all = w_in[i].astype(BF16)
        w_dt = jnp.pad(w_in[i, :, n_main:], pad_h).astype(BF16)
        dtb = jnp.pad(dt_bias[i][None], pad_h)
        alog = jnp.pad(a_log[i][None], pad_h)
        d_skip_e = jnp.repeat(d_skip[i], hp)[None]

        pa, dt_raw, u = _norm_proj(h, norm_mix_g[i][None], w_all, w_dt, col0=0, n_tiles=off_z // tn,
                                   tm=tm_proj, tn=tn, epilogue="none")
        z_act = _proj(u, w_all, col0=off_z // tn, n_tiles=d_inner // tn,
                      tm=tm_proj, tn=tn, epilogue="silu")
        xbc = _proj_conv(u, w_all, ssm_conv_w[i], ssm_conv_b[i][None], col0=off_xbc // tn,
                         n_tiles=conv_dim // tn, seq=seq, tm=tm_proj, tn=tn)

        ya = _mixer_a(pa, conv_a_w[i], w_out_a[i].astype(BF16), seq=seq,
                      tm=_pick(seq, (512, 256, 128)), tile_gate=0, tile_b=off_a // tn)
        cs, rt, wt = _ssd_prep(dt_raw, dtb, alog, tr=tm_proj)
        y = _ssd(xbc, cs, rt, wt, d_skip_e, batch=batch, seq=seq,
                 chunks_per_step=_pick(seq // SSM_CHUNK, (SSD_CHUNKS_PER_STEP, 2, 1)))
        mixed = _mix(y, z_act, pa, ya, ssm_norm_g[i][None], w_out_ssm[i].astype(BF16),
                     tm=_pick(m, (512, 256, 128)), tile_gate_b=1)
        h = _out_proj(mixed, h, w_o[i].astype(BF16), tm=_pick(m, (1024, 512, 256, 128)))
        h = _ffn(h, norm_ffn_g[i][None], w_ffn_gate[i].astype(BF16), w_ffn_up[i].astype(BF16),
                 w_ffn_down[i].astype(BF16), norm_final_g[None],
                 tm=_pick(m, (1024, 512, 256, 128)), tf=_pick(w_ffn_gate.shape[2], (512, 256, 128)),
                 final_norm=(i == depth - 1))
    return h.reshape(batch, seq, d)
```

```python
import functools

import jax
import jax.numpy as jnp
from jax import lax
from jax.experimental import pallas as pl
from jax.experimental.pallas import tpu as pltpu

EPS = 1e-6
LOG2_E = 1.4426950408889634
SSM_GROUPS = 8
SSM_STATE = 128
SSM_CHUNK = 128
SSD_CHUNKS_PER_STEP = 4

LANES = 128
BF16_SUBLANES = 16
F32_SUBLANES = 8
V7X_VMEM_LIMIT_BYTES = 60000 * 1024

F32 = jnp.float32
BF16 = jnp.bfloat16


def _params(semantics):
    return pltpu.CompilerParams(dimension_semantics=semantics,
                                vmem_limit_bytes=V7X_VMEM_LIMIT_BYTES)


def _resident(shape):
    return pl.BlockSpec(shape, lambda *_: (0,) * len(shape), pipeline_mode=pl.Buffered(1))


def _rmsnorm(x, g):
    return x * lax.rsqrt(jnp.mean(x * x, axis=-1, keepdims=True) + EPS) * g


def _dot(a, b):
    return jnp.dot(a, b, preferred_element_type=F32)


def _sigmoid(x):
    return 1.0 / (1.0 + jnp.exp2(x * -LOG2_E))


def _store_projection(res, out_ref, epilogue):
    if epilogue == "silu":
        res = res * _sigmoid(res)
    else:
        assert epilogue == "none"
    out_ref[...] = res.astype(BF16)


def _tile_major(m, n_tiles, tm, tn):
    return (jax.ShapeDtypeStruct((n_tiles, m, tn), BF16),
            pl.BlockSpec((None, tm, tn), lambda i, j: (j, i, 0)))


def _norm_proj_kernel(x_ref, g_ref, w_ref, wdt_ref, out_ref, dt_ref, u_ref, *, epilogue):
    @pl.when(pl.program_id(1) == 0)
    def _():
        u = _rmsnorm(x_ref[...], g_ref[...]).astype(BF16)
        u_ref[...] = u
        dt_ref[...] = _dot(u, wdt_ref[...])

    _store_projection(_dot(u_ref[...], w_ref[...]), out_ref, epilogue)


def _norm_proj(x2, g, w_all, w_dt, *, col0, n_tiles, tm, tn, epilogue):
    m, d = x2.shape
    out_shape, out_spec = _tile_major(m, n_tiles, tm, tn)
    return pl.pallas_call(
        functools.partial(_norm_proj_kernel, epilogue=epilogue),
        grid=(m // tm, n_tiles),
        in_specs=[
            pl.BlockSpec((tm, d), lambda i, j: (i, 0)),
            pl.BlockSpec((1, d), lambda i, j: (0, 0)),
            pl.BlockSpec((d, tn), lambda i, j: (0, col0 + j)),
            pl.BlockSpec((d, LANES), lambda i, j: (0, 0)),
        ],
        out_specs=[
            out_spec,
            pl.BlockSpec((tm, LANES), lambda i, j: (i, 0)),
            pl.BlockSpec((tm, d), lambda i, j: (i, 0)),
        ],
        out_shape=[out_shape,
                   jax.ShapeDtypeStruct((m, LANES), F32),
                   jax.ShapeDtypeStruct((m, d), BF16)],
        compiler_params=_params(("arbitrary", "arbitrary")),
        name="norm_proj_" + epilogue,
    )(x2, g, w_all, w_dt)


def _proj_kernel(u_ref, w_ref, out_ref, *, epilogue):
    _store_projection(_dot(u_ref[...], w_ref[...]), out_ref, epilogue)


def _proj(u, w_all, *, col0, n_tiles, tm, tn, epilogue):
    m, d = u.shape
    out_shape, out_spec = _tile_major(m, n_tiles, tm, tn)
    return pl.pallas_call(
        functools.partial(_proj_kernel, epilogue=epilogue),
        grid=(m // tm, n_tiles),
        in_specs=[pl.BlockSpec((tm, d), lambda i, j: (i, 0)),
                  pl.BlockSpec((d, tn), lambda i, j: (0, col0 + j))],
        out_specs=out_spec,
        out_shape=out_shape,
        compiler_params=_params(("arbitrary", "arbitrary")),
        name="proj_" + epilogue,
    )(u, w_all)


def _proj_product_kernel(u_ref, w_ref, out_ref, first_ref):
    j = pl.program_id(1)

    @pl.when(j == 0)
    def _():
        first_ref[...] = _dot(u_ref[...], w_ref[...])

    @pl.when(j == 1)
    def _():
        out_ref[...] = (first_ref[...] * _dot(u_ref[...], w_ref[...])).astype(BF16)


def _proj_product(u, w_all, *, col0, tm, tn):
    m, d = u.shape
    return pl.pallas_call(
        _proj_product_kernel,
        grid=(m // tm, 2),
        in_specs=[pl.BlockSpec((tm, d), lambda i, j: (i, 0)),
                  pl.BlockSpec((d, tn), lambda i, j: (0, col0 + j))],
        out_specs=pl.BlockSpec((tm, tn), lambda i, j: (i, 0)),
        out_shape=jax.ShapeDtypeStruct((m, tn), BF16),
        scratch_shapes=[pltpu.VMEM((tm, tn), F32)],
        compiler_params=_params(("arbitrary", "arbitrary")),
        name="proj_product",
    )(u, w_all)


def _proj_conv_kernel(u_ref, w_ref, cw_ref, cb_ref, out_ref, buf_ref, carry_ref, *, tiles_per_seq):
    tm = u_ref.shape[0]
    taps = cw_ref.shape[0]
    tail = F32_SUBLANES
    j = pl.program_id(1)
    seq_start = pl.program_id(0) % tiles_per_seq == 0
    res = _dot(u_ref[...], w_ref[...])
    for c in range(buf_ref.shape[0]):
        lanes = slice(c * LANES, (c + 1) * LANES)
        res_c = res[:, lanes]
        buf_ref[c, 0:tail, :] = jnp.where(seq_start, 0.0, carry_ref[j, c])
        buf_ref[c, tail:, :] = res_c
        acc = cb_ref[:, lanes] + cw_ref[taps - 1:taps, lanes] * res_c
        for k in range(taps - 1):
            lo = tail - (taps - 1 - k)
            acc = acc + cw_ref[k:k + 1, lanes] * buf_ref[c, lo:lo + tm, :]
        carry_ref[j, c] = buf_ref[c, tm:tm + tail, :]
        out_ref[:, lanes] = (acc * _sigmoid(acc)).astype(BF16)


def _proj_conv(u, w_all, conv_w, conv_b, *, col0, n_tiles, seq, tm, tn):
    m, d = u.shape
    taps = conv_w.shape[0]
    out_shape, out_spec = _tile_major(m, n_tiles, tm, tn)
    return pl.pallas_call(
        functools.partial(_proj_conv_kernel, tiles_per_seq=seq // tm),
        grid=(m // tm, n_tiles),
        in_specs=[pl.BlockSpec((tm, d), lambda i, j: (i, 0)),
                  pl.BlockSpec((d, tn), lambda i, j: (0, col0 + j)),
                  pl.BlockSpec((taps, tn), lambda i, j: (0, j)),
                  pl.BlockSpec((1, tn), lambda i, j: (0, j))],
        out_specs=out_spec,
        out_shape=out_shape,
        scratch_shapes=[pltpu.VMEM((tn // LANES, F32_SUBLANES + tm, LANES), F32),
                        pltpu.VMEM((n_tiles, tn // LANES, F32_SUBLANES, LANES), F32)],
        compiler_params=_params(("arbitrary", "arbitrary")),
        name="proj_conv",
    )(u, w_all, conv_w, conv_b)


def _mixer_a_kernel(ga_ref, b_ref, v_ref, cw_ref, wo_ref, out_ref,
                    buf_ref, carry_ref, y_ref, *, tiles_per_seq):
    tm = v_ref.shape[0]
    taps = cw_ref.shape[0]
    tail = F32_SUBLANES
    seq_start = pl.program_id(0) % tiles_per_seq == 0
    for s in range(buf_ref.shape[0]):
        lanes = slice(s * LANES, (s + 1) * LANES)
        v = v_ref[:, lanes].astype(F32)
        buf_ref[s, 0:tail, :] = jnp.where(seq_start, 0.0, carry_ref[s])
        buf_ref[s, tail:, :] = v
        acc = cw_ref[taps - 1:taps, lanes] * v
        for j in range(taps - 1):
            lo = tail - (taps - 1 - j)
            acc = acc + cw_ref[j:j + 1, lanes] * buf_ref[s, lo:lo + tm, :]
        carry_ref[s] = buf_ref[s, tm:tm + tail, :]
        y_ref[:, lanes] = (b_ref[:, lanes].astype(F32) * acc).astype(BF16)
    out_ref[...] = (_sigmoid(ga_ref[...].astype(F32)) * _dot(y_ref[...], wo_ref[...])).astype(BF16)


def _mixer_a(pa, v, conv_w, w_out, *, seq, tm, tile_gate, tile_b):
    m = pa.shape[1]
    c = conv_w.shape[1]
    d = w_out.shape[1]

    def col(k):
        return pl.BlockSpec((None, tm, c), lambda i: (k, i, 0))

    return pl.pallas_call(
        functools.partial(_mixer_a_kernel, tiles_per_seq=seq // tm),
        grid=(m // tm,),
        in_specs=[col(tile_gate), col(tile_b), pl.BlockSpec((tm, c), lambda i: (i, 0)),
                  _resident(conv_w.shape), _resident(w_out.shape)],
        out_specs=pl.BlockSpec((tm, d), lambda i: (i, 0)),
        out_shape=jax.ShapeDtypeStruct((m, d), BF16),
        scratch_shapes=[pltpu.VMEM((c // LANES, F32_SUBLANES + tm, LANES), F32),
                        pltpu.VMEM((c // LANES, F32_SUBLANES, LANES), F32),
                        pltpu.VMEM((tm, c), BF16)],
        compiler_params=_params(("arbitrary",)),
        name="mixer_a",
    )(pa, pa, v, conv_w, w_out)


def _ssd_prep_kernel(dt_ref, dtb_ref, alog_ref, cs_ref, rt_ref, wt_ref):
    q = SSM_CHUNK
    row = lax.broadcasted_iota(jnp.int32, (q, q), 0)
    col = lax.broadcasted_iota(jnp.int32, (q, q), 1)
    tril = (col <= row).astype(F32)
    a = -jnp.exp(alog_ref[...])
    for k in range(rt_ref.shape[0]):
        rows = slice(k * q, (k + 1) * q)
        dt = jax.nn.softplus(dt_ref[rows, :] + dtb_ref[...])
        cs = LOG2_E * jnp.dot(tril, dt * a, preferred_element_type=F32,
                              precision=lax.Precision.HIGHEST)
        r = cs - LOG2_E * jnp.log(dt)
        cs_ref[rows, :] = cs
        rt_ref[k] = r.T
        wt_ref[k] = jnp.exp2(cs[q - 1:q, :] - r).T


def _ssd_prep(dt_raw, dt_bias, a_log, *, tr):
    m = dt_raw.shape[0]
    q = SSM_CHUNK
    per_chunk = jax.ShapeDtypeStruct((m // q, LANES, q), F32)
    return pl.pallas_call(
        _ssd_prep_kernel,
        grid=(m // tr,),
        in_specs=[pl.BlockSpec((tr, LANES), lambda i: (i, 0)),
                  pl.BlockSpec(dt_bias.shape, lambda i: (0, 0)),
                  pl.BlockSpec(a_log.shape, lambda i: (0, 0))],
        out_specs=[pl.BlockSpec((tr, LANES), lambda i: (i, 0)),
                   pl.BlockSpec((tr // q, LANES, q), lambda i: (i, 0, 0)),
                   pl.BlockSpec((tr // q, LANES, q), lambda i: (i, 0, 0))],
        out_shape=[jax.ShapeDtypeStruct((m, LANES), F32), per_chunk, per_chunk],
        compiler_params=_params(("arbitrary",)),
        name="ssd_prep",
    )(dt_raw, dt_bias, a_log)


def _ssd_kernel(x0_ref, x1_ref, bc_ref, z0_ref, z1_ref, cs_ref, rt_ref, wt_ref, dsk_ref, y_ref, st_ref):
    q = SSM_CHUNK
    hp = LANES // 2

    @pl.when(pl.program_id(1) == 0)
    def _():
        st_ref[...] = jnp.zeros(st_ref.shape, F32)

    row = lax.broadcasted_iota(jnp.int32, (q, q), 0)
    col = lax.broadcasted_iota(jnp.int32, (q, q), 1)
    causal = col <= row
    left = lax.broadcasted_iota(jnp.int32, (q, LANES), 1) < hp
    for k in range(rt_ref.shape[0]):
        _ssd_chunk(k, x0_ref, x1_ref, bc_ref, z0_ref, z1_ref, cs_ref, rt_ref, wt_ref, dsk_ref, y_ref,
                   st_ref, causal, left)


def _ssd_chunk(k, x0_ref, x1_ref, bc_ref, z0_ref, z1_ref, cs_ref, rt_ref, wt_ref, dsk_ref, y_ref,
               st_ref, causal, left):
    q = SSM_CHUNK
    groups = SSM_GROUPS
    pairs_g = st_ref.shape[0] // groups
    half = x0_ref.shape[1]
    rows = slice(k * q, (k + 1) * q)
    for g in range(groups):
        bg = bc_ref[rows, g * LANES:(g + 1) * LANES]
        cg = bc_ref[rows, (groups + g) * LANES:(groups + g + 1) * LANES]
        cbm = lax.dot_general(cg, bg, (((1,), (1,)), ((), ())),
                              preferred_element_type=F32).astype(BF16)
        bt = bg.T.reshape(SSM_STATE // BF16_SUBLANES, BF16_SUBLANES, q)
        st_g = jnp.concatenate([st_ref[g * pairs_g + p] for p in range(pairs_g)], axis=1)
        y_off = _dot(cg, st_g.astype(BF16))
        for p in range(pairs_g):
            slab = g * pairs_g + p
            lanes = slice(slab * LANES, (slab + 1) * LANES)
            lhs_y, lhs_s, a = [], [], []
            for h in (2 * slab, 2 * slab + 1):
                a_h = cs_ref[rows, h:h + 1]
                a.append(a_h)
                seg = a_h - rt_ref[k, h:h + 1, :]
                decay = jnp.exp2(jnp.where(causal, seg, -jnp.inf))
                lhs_y.append(cbm * decay.astype(BF16))
                w_h = jnp.broadcast_to(wt_ref[k, h:h + 1, :], (BF16_SUBLANES, q)).astype(BF16)
                lhs_s.append((bt * w_h[None]).reshape(SSM_STATE, q))
            lhs = jnp.concatenate([jnp.concatenate(lhs_y, axis=1),
                                   jnp.concatenate(lhs_s, axis=1)], axis=0)
            src = x0_ref if slab * LANES < half else x1_ref
            off = (slab * LANES) % half
            xs = src[rows, off:off + LANES].astype(F32)
            rhs = jnp.concatenate([jnp.where(left, xs, 0.0).astype(BF16),
                                   jnp.where(left, 0.0, xs).astype(BF16)], axis=0)
            res = _dot(lhs, rhs)
            e12 = jnp.exp2(jnp.where(left, a[0], a[1]))
            st_ref[slab] = st_ref[slab] * e12[q - 1:q, :] + res[q:, :]
            y = res[:q, :] + y_off[:, p * LANES:(p + 1) * LANES] * e12 + dsk_ref[:, lanes] * xs
            zsrc = z0_ref if slab * LANES < half else z1_ref
            y_ref[rows, lanes] = (y * zsrc[rows, off:off + LANES].astype(F32)).astype(BF16)


def _ssd(xbc, z_act, cs, rt, wt, d_skip_e, *, batch, seq, chunks_per_step):
    n_tiles, m, tn = xbc.shape
    assert n_tiles == 3 and tn == 2 * SSM_GROUPS * SSM_STATE and z_act.shape == (2, m, tn)
    d_inner = 2 * tn
    q = SSM_CHUNK
    tr = chunks_per_step * q
    ns = seq // tr

    def tile(k):
        return pl.BlockSpec((None, tr, tn), lambda b, c: (k, b * ns + c, 0))

    def rows(width):
        return pl.BlockSpec((tr, width), lambda b, c: (b * ns + c, 0))

    per_chunk = pl.BlockSpec((chunks_per_step, LANES, q), lambda b, c: (b * ns + c, 0, 0))
    return pl.pallas_call(
        _ssd_kernel,
        grid=(batch, ns),
        in_specs=[tile(0), tile(1), tile(2), tile(0), tile(1), rows(LANES), per_chunk, per_chunk,
                  pl.BlockSpec(d_skip_e.shape, lambda b, c: (0, 0))],
        out_specs=rows(d_inner),
        out_shape=jax.ShapeDtypeStruct((m, d_inner), BF16),
        scratch_shapes=[pltpu.VMEM((d_inner // LANES, SSM_STATE, LANES), F32)],
        compiler_params=_params(("arbitrary", "arbitrary")),
        name="ssd",
    )(xbc, xbc, xbc, z_act, z_act, cs, rt, wt, d_skip_e)


def _mix_kernel(yz_in_ref, gb_ref, ya_ref, ng_ref, wssm_ref, out_ref, yz_ref):
    gw = yz_in_ref.shape[1] // SSM_GROUPS
    for g in range(SSM_GROUPS):
        cols = slice(g * gw, (g + 1) * gw)
        yz = yz_in_ref[:, cols].astype(F32)
        yz = yz * lax.rsqrt(jnp.mean(yz * yz, axis=-1, keepdims=True) + EPS)
        yz_ref[:, cols] = (yz * ng_ref[:, cols]).astype(BF16)

    yb = _dot(yz_ref[...], wssm_ref[...])
    out_ref[...] = (ya_ref[...].astype(F32) + _sigmoid(gb_ref[...].astype(F32)) * yb).astype(BF16)


def _mix(yz, pa, ya, norm_g, w_ssm, *, tm, tile_gate_b):
    m, di = yz.shape
    d = w_ssm.shape[1]
    assert pa.shape[2] == d

    def tile(width, k):
        return pl.BlockSpec((None, tm, width), lambda i: (k, i, 0))

    def rows(width):
        return pl.BlockSpec((tm, width), lambda i: (i, 0))

    return pl.pallas_call(
        _mix_kernel,
        grid=(m // tm,),
        in_specs=[rows(di), tile(d, tile_gate_b), rows(d),
                  _resident(norm_g.shape), _resident(w_ssm.shape)],
        out_specs=rows(d),
        out_shape=jax.ShapeDtypeStruct((m, d), BF16),
        scratch_shapes=[pltpu.VMEM((tm, di), BF16)],
        compiler_params=_params(("arbitrary",)),
        name="mix",
    )(yz, pa, ya, norm_g, w_ssm)


def _out_proj_kernel(mixed_ref, x_ref, wo_ref, h_ref):
    h_ref[...] = x_ref[...] + _dot(mixed_ref[...], wo_ref[...])


def _out_proj(mixed, x2, w_o, *, tm):
    m, d = x2.shape

    def rows():
        return pl.BlockSpec((tm, d), lambda i: (i, 0))

    return pl.pallas_call(
        _out_proj_kernel,
        grid=(m // tm,),
        in_specs=[rows(), rows(), _resident(w_o.shape)],
        out_specs=rows(),
        out_shape=jax.ShapeDtypeStruct((m, d), F32),
        compiler_params=_params(("arbitrary",)),
        name="out_proj",
    )(mixed, x2, w_o)


def _ffn_kernel(h_ref, g_ref, wg_ref, wu_ref, wd_ref, gf_ref, out_ref, u_ref, *, final_norm):
    f = pl.program_id(1)

    @pl.when(f == 0)
    def _():
        h = h_ref[...]
        u_ref[...] = _rmsnorm(h, g_ref[...]).astype(BF16)
        out_ref[...] = h

    u = u_ref[...]
    tf = wg_ref.shape[1]
    part = None
    for lo in range(0, tf, tf // 2):
        cols = slice(lo, lo + tf // 2)
        gate = _dot(u, wg_ref[:, cols])
        act = (gate * _sigmoid(gate) * _dot(u, wu_ref[:, cols])).astype(BF16)
        p = _dot(act, wd_ref[cols, :])
        part = p if part is None else part + p
    out_ref[...] += part

    if final_norm:
        @pl.when(f == pl.num_programs(1) - 1)
        def _():
            out_ref[...] = _rmsnorm(out_ref[...], gf_ref[...])


def _ffn(h, g_ffn, w_gate, w_up, w_down, g_final, *, tm, tf, final_norm):
    m, d = h.shape
    ff = w_gate.shape[1]
    return pl.pallas_call(
        functools.partial(_ffn_kernel, final_norm=final_norm),
        grid=(m // tm, ff // tf),
        in_specs=[pl.BlockSpec((tm, d), lambda i, f: (i, 0)),
                  pl.BlockSpec((1, d), lambda i, f: (0, 0)),
                  pl.BlockSpec((d, tf), lambda i, f: (0, f)),
                  pl.BlockSpec((d, tf), lambda i, f: (0, f)),
                  pl.BlockSpec((tf, d), lambda i, f: (f, 0)),
                  pl.BlockSpec((1, d), lambda i, f: (0, 0))],
        out_specs=pl.BlockSpec((tm, d), lambda i, f: (i, 0)),
        out_shape=jax.ShapeDtypeStruct((m, d), F32),
        scratch_shapes=[pltpu.VMEM((tm, d), BF16)],
        compiler_params=_params(("arbitrary", "arbitrary")),
        name="ffn",
    )(h, g_ffn, w_gate, w_up, w_down, g_final)


def _pick(n, candidates):
    for c in candidates:
        if n % c == 0:
            return c
    raise ValueError(f"no tile in {candidates} divides {n}")


def kernel(x, norm_mix_g, w_in, conv_a_w, w_out_a, ssm_conv_w, ssm_conv_b, dt_bias, a_log, d_skip,
           ssm_norm_g, w_out_ssm, w_o, norm_ffn_g, w_ffn_gate, w_ffn_up, w_ffn_down, norm_final_g):
    batch, seq, d = x.shape
    m = batch * seq
    depth = w_in.shape[0]
    d_inner = w_out_ssm.shape[1]
    heads = dt_bias.shape[1]
    gn = SSM_GROUPS * SSM_STATE
    conv_dim = d_inner + 2 * gn
    n_main = w_in.shape[2] - heads
    off_a = 2 * d
    off_z = off_a + 3 * conv_a_w.shape[2]
    off_xbc = off_z + d_inner
    hp = d_inner // heads
    tn = d
    assert conv_a_w.shape[2] == d and off_xbc + conv_dim == n_main
    assert d_inner == 2 * tn and 2 * gn == tn
    assert seq % SSM_CHUNK == 0 and heads <= LANES and heads % (2 * SSM_GROUPS) == 0
    assert 2 * hp == LANES and SSM_STATE == LANES and SSM_CHUNK == LANES
    pad_h = ((0, 0), (0, LANES - heads))
    tm_proj = _pick(seq, (1024, 512, 256, 128))

    h = x.reshape(m, d)
    for i in range(depth):
        w_all = w_in[i].astype(BF16)
        w_dt = jnp.pad(w_in[i, :, n_main:], pad_h).astype(BF16)
        dtb = jnp.pad(dt_bias[i][None], pad_h)
        alog = jnp.pad(a_log[i][None], pad_h)
        d_skip_e = jnp.repeat(d_skip[i], hp)[None]

        pa, dt_raw, u = _norm_proj(h, norm_mix_g[i][None], w_all, w_dt, col0=0, n_tiles=off_a // tn + 1,
                                   tm=tm_proj, tn=tn, epilogue="none")
        v = _proj_product(u, w_all, col0=off_a // tn + 1, tm=tm_proj, tn=tn)
        z_act = _proj(u, w_all, col0=off_z // tn, n_tiles=d_inner // tn,
                      tm=tm_proj, tn=tn, epilogue="silu")
        xbc = _proj_conv(u, w_all, ssm_conv_w[i], ssm_conv_b[i][None], col0=off_xbc // tn,
                         n_tiles=conv_dim // tn, seq=seq, tm=tm_proj, tn=tn)

        ya = _mixer_a(pa, v, conv_a_w[i], w_out_a[i].astype(BF16), seq=seq,
                      tm=_pick(seq, (512, 256, 128)), tile_gate=0, tile_b=off_a // tn)
        cs, rt, wt = _ssd_prep(dt_raw, dtb, alog, tr=tm_proj)
        yz = _ssd(xbc, z_act, cs, rt, wt, d_skip_e, batch=batch, seq=seq,
                  chunks_per_step=_pick(seq // SSM_CHUNK, (SSD_CHUNKS_PER_STEP, 2, 1)))
        mixed = _mix(yz, pa, ya, ssm_norm_g[i][None], w_out_ssm[i].astype(BF16),
                     tm=_pick(m, (512, 256, 128)), tile_gate_b=1)
        h = _out_proj(mixed, h, w_o[i].astype(BF16), tm=_pick(m, (1024, 512, 256, 128)))
        h = _ffn(h, norm_ffn_g[i][None], w_ffn_gate[i].astype(BF16), w_ffn_up[i].astype(BF16),
                 w_ffn_down[i].astype(BF16), norm_final_g[None],
                 tm=_pick(m, (1024, 512, 256, 128)), tf=_pick(w_ffn_gate.shape[2], (512, 256, 128)),
                 final_norm=(i == depth - 1))
    return h.reshape(batch, seq, d)
```

```python
import functools

import jax
import jax.numpy as jnp
from jax import lax
from jax.experimental import pallas as pl
from jax.experimental.pallas import tpu as pltpu

EPS = 1e-6
LOG2_E = 1.4426950408889634
SSM_GROUPS = 8
SSM_STATE = 128
SSM_CHUNK = 128
SSD_CHUNKS_PER_STEP = 4

LANES = 128
BF16_SUBLANES = 16
F32_SUBLANES = 8
V7X_VMEM_LIMIT_BYTES = 60000 * 1024

F32 = jnp.float32
BF16 = jnp.bfloat16


def _params(semantics):
    return pltpu.CompilerParams(dimension_semantics=semantics,
                                vmem_limit_bytes=V7X_VMEM_LIMIT_BYTES)


def _resident(shape):
    return pl.BlockSpec(shape, lambda *_: (0,) * len(shape), pipeline_mode=pl.Buffered(1))


def _rmsnorm(x, g):
    return x * lax.rsqrt(jnp.mean(x * x, axis=-1, keepdims=True) + EPS) * g


def _dot(a, b):
    return jnp.dot(a, b, preferred_element_type=F32)


def _sigmoid(x):
    return 1.0 / (1.0 + jnp.exp2(x * -LOG2_E))


def _store_projection(res, out_ref, epilogue):
    if epilogue == "silu":
        res = res * _sigmoid(res)
    else:
        assert epilogue == "none"
    out_ref[...] = res.astype(BF16)


def _tile_major(m, n_tiles, tm, tn):
    return (jax.ShapeDtypeStruct((n_tiles, m, tn), BF16),
            pl.BlockSpec((None, tm, tn), lambda i, j: (j, i, 0)))


def _norm_proj_kernel(x_ref, g_ref, w_ref, wdt_ref, out_ref, dt_ref, u_ref, *, epilogue):
    @pl.when(pl.program_id(1) == 0)
    def _():
        u = _rmsnorm(x_ref[...], g_ref[...]).astype(BF16)
        u_ref[...] = u
        dt_ref[...] = _dot(u, wdt_ref[...])

    _store_projection(_dot(u_ref[...], w_ref[...]), out_ref, epilogue)


def _norm_proj(x2, g, w_all, w_dt, *, col0, n_tiles, tm, tn, epilogue):
    m, d = x2.shape
    out_shape, out_spec = _tile_major(m, n_tiles, tm, tn)
    return pl.pallas_call(
        functools.partial(_norm_proj_kernel, epilogue=epilogue),
        grid=(m // tm, n_tiles),
        in_specs=[
            pl.BlockSpec((tm, d), lambda i, j: (i, 0)),
            pl.BlockSpec((1, d), lambda i, j: (0, 0)),
            pl.BlockSpec((d, tn), lambda i, j: (0, col0 + j)),
            pl.BlockSpec((d, LANES), lambda i, j: (0, 0)),
        ],
        out_specs=[
            out_spec,
            pl.BlockSpec((tm, LANES), lambda i, j: (i, 0)),
            pl.BlockSpec((tm, d), lambda i, j: (i, 0)),
        ],
        out_shape=[out_shape,
                   jax.ShapeDtypeStruct((m, LANES), F32),
                   jax.ShapeDtypeStruct((m, d), BF16)],
        compiler_params=_params(("arbitrary", "arbitrary")),
        name="norm_proj_" + epilogue,
    )(x2, g, w_all, w_dt)


def _proj_kernel(u_ref, w_ref, out_ref, *, epilogue):
    _store_projection(_dot(u_ref[...], w_ref[...]), out_ref, epilogue)


def _proj(u, w_all, *, col0, n_tiles, tm, tn, epilogue):
    m, d = u.shape
    out_shape, out_spec = _tile_major(m, n_tiles, tm, tn)
    return pl.pallas_call(
        functools.partial(_proj_kernel, epilogue=epilogue),
        grid=(m // tm, n_tiles),
        in_specs=[pl.BlockSpec((tm, d), lambda i, j: (i, 0)),
                  pl.BlockSpec((d, tn), lambda i, j: (0, col0 + j))],
        out_specs=out_spec,
        out_shape=out_shape,
        compiler_params=_params(("arbitrary", "arbitrary")),
        name="proj_" + epilogue,
    )(u, w_all)


def _proj_conv_kernel(u_ref, w_ref, cw_ref, cb_ref, out_ref, buf_ref, carry_ref, *, tiles_per_seq):
    tm = u_ref.shape[0]
    taps = cw_ref.shape[0]
    tail = F32_SUBLANES
    j = pl.program_id(1)
    seq_start = pl.program_id(0) % tiles_per_seq == 0
    res = _dot(u_ref[...], w_ref[...])
    for c in range(buf_ref.shape[0]):
        lanes = slice(c * LANES, (c + 1) * LANES)
        res_c = res[:, lanes]
        buf_ref[c, 0:tail, :] = jnp.where(seq_start, 0.0, carry_ref[j, c])
        buf_ref[c, tail:, :] = res_c
        acc = cb_ref[:, lanes] + cw_ref[taps - 1:taps, lanes] * res_c
        for k in range(taps - 1):
            lo = tail - (taps - 1 - k)
            acc = acc + cw_ref[k:k + 1, lanes] * buf_ref[c, lo:lo + tm, :]
        carry_ref[j, c] = buf_ref[c, tm:tm + tail, :]
        out_ref[:, lanes] = (acc * _sigmoid(acc)).astype(BF16)


def _proj_conv(u, w_all, conv_w, conv_b, *, col0, n_tiles, seq, tm, tn):
    m, d = u.shape
    taps = conv_w.shape[0]
    out_shape, out_spec = _tile_major(m, n_tiles, tm, tn)
    return pl.pallas_call(
        functools.partial(_proj_conv_kernel, tiles_per_seq=seq // tm),
        grid=(m // tm, n_tiles),
        in_specs=[pl.BlockSpec((tm, d), lambda i, j: (i, 0)),
                  pl.BlockSpec((d, tn), lambda i, j: (0, col0 + j)),
                  pl.BlockSpec((taps, tn), lambda i, j: (0, j)),
                  pl.BlockSpec((1, tn), lambda i, j: (0, j))],
        out_specs=out_spec,
        out_shape=out_shape,
        scratch_shapes=[pltpu.VMEM((tn // LANES, F32_SUBLANES + tm, LANES), F32),
                        pltpu.VMEM((n_tiles, tn // LANES, F32_SUBLANES, LANES), F32)],
        compiler_params=_params(("arbitrary", "arbitrary")),
        name="proj_conv",
    )(u, w_all, conv_w, conv_b)


def _mixer_a_kernel(ga_ref, b_ref, c_ref, h_ref, cw_ref, wo_ref, out_ref,
                    carry_ref, y_ref, *, tiles_per_seq):
    tm = c_ref.shape[0]
    taps = cw_ref.shape[0]
    tail = F32_SUBLANES
    seq_start = pl.program_id(0) % tiles_per_seq == 0
    for s in range(carry_ref.shape[0]):
        lanes = slice(s * LANES, (s + 1) * LANES)
        v = c_ref[:, lanes].astype(F32) * h_ref[:, lanes].astype(F32)
        ext = jnp.concatenate([jnp.where(seq_start, 0.0, carry_ref[s]), v], axis=0)
        acc = cw_ref[taps - 1:taps, lanes] * v
        for j in range(taps - 1):
            lo = tail - (taps - 1 - j)
            acc = acc + cw_ref[j:j + 1, lanes] * ext[lo:lo + tm, :]
        carry_ref[s] = v[tm - tail:, :]
        y_ref[:, lanes] = (b_ref[:, lanes].astype(F32) * acc).astype(BF16)
    out_ref[...] = (_sigmoid(ga_ref[...].astype(F32)) * _dot(y_ref[...], wo_ref[...])).astype(BF16)


def _mixer_a(pa, conv_w, w_out, *, seq, tm, tile_gate, tile_b):
    m = pa.shape[1]
    c = conv_w.shape[1]
    d = w_out.shape[1]

    def col(k):
        return pl.BlockSpec((None, tm, c), lambda i: (k, i, 0))

    return pl.pallas_call(
        functools.partial(_mixer_a_kernel, tiles_per_seq=seq // tm),
        grid=(m // tm,),
        in_specs=[col(tile_gate), col(tile_b), col(tile_b + 1), col(tile_b + 2),
                  _resident(conv_w.shape), _resident(w_out.shape)],
        out_specs=pl.BlockSpec((tm, d), lambda i: (i, 0)),
        out_shape=jax.ShapeDtypeStruct((m, d), BF16),
        scratch_shapes=[pltpu.VMEM((c // LANES, F32_SUBLANES, LANES), F32),
                        pltpu.VMEM((tm, c), BF16)],
        compiler_params=_params(("arbitrary",)),
        name="mixer_a",
    )(pa, pa, pa, pa, conv_w, w_out)


def _ssd_prep_kernel(dt_ref, dtb_ref, alog_ref, cs_ref, rt_ref, wt_ref):
    q = SSM_CHUNK
    row = lax.broadcasted_iota(jnp.int32, (q, q), 0)
    col = lax.broadcasted_iota(jnp.int32, (q, q), 1)
    tril = (col <= row).astype(F32)
    a = -jnp.exp(alog_ref[...])
    for k in range(rt_ref.shape[0]):
        rows = slice(k * q, (k + 1) * q)
        dt = jax.nn.softplus(dt_ref[rows, :] + dtb_ref[...])
        cs = LOG2_E * jnp.dot(tril, dt * a, preferred_element_type=F32,
                              precision=lax.Precision.HIGHEST)
        r = cs - LOG2_E * jnp.log(dt)
        cs_ref[rows, :] = cs
        rt_ref[k] = r.T
        wt_ref[k] = jnp.exp2(cs[q - 1:q, :] - r).T


def _ssd_prep(dt_raw, dt_bias, a_log, *, tr):
    m = dt_raw.shape[0]
    q = SSM_CHUNK
    per_chunk = jax.ShapeDtypeStruct((m // q, LANES, q), F32)
    return pl.pallas_call(
        _ssd_prep_kernel,
        grid=(m // tr,),
        in_specs=[pl.BlockSpec((tr, LANES), lambda i: (i, 0)),
                  pl.BlockSpec(dt_bias.shape, lambda i: (0, 0)),
                  pl.BlockSpec(a_log.shape, lambda i: (0, 0))],
        out_specs=[pl.BlockSpec((tr, LANES), lambda i: (i, 0)),
                   pl.BlockSpec((tr // q, LANES, q), lambda i: (i, 0, 0)),
                   pl.BlockSpec((tr // q, LANES, q), lambda i: (i, 0, 0))],
        out_shape=[jax.ShapeDtypeStruct((m, LANES), F32), per_chunk, per_chunk],
        compiler_params=_params(("arbitrary",)),
        name="ssd_prep",
    )(dt_raw, dt_bias, a_log)


def _ssd_kernel(x0_ref, x1_ref, bc_ref, cs_ref, rt_ref, wt_ref, dsk_ref, y_ref, st_ref):
    q = SSM_CHUNK
    hp = LANES // 2

    @pl.when(pl.program_id(1) == 0)
    def _():
        st_ref[...] = jnp.zeros(st_ref.shape, F32)

    row = lax.broadcasted_iota(jnp.int32, (q, q), 0)
    col = lax.broadcasted_iota(jnp.int32, (q, q), 1)
    causal = col <= row
    left = lax.broadcasted_iota(jnp.int32, (q, LANES), 1) < hp
    for k in range(rt_ref.shape[0]):
        _ssd_chunk(k, x0_ref, x1_ref, bc_ref, cs_ref, rt_ref, wt_ref, dsk_ref, y_ref, st_ref,
                   causal, left)


def _ssd_chunk(k, x0_ref, x1_ref, bc_ref, cs_ref, rt_ref, wt_ref, dsk_ref, y_ref, st_ref, causal, left):
    q = SSM_CHUNK
    groups = SSM_GROUPS
    pairs_g = st_ref.shape[0] // groups
    half = x0_ref.shape[1]
    rows = slice(k * q, (k + 1) * q)
    for g in range(groups):
        bg = bc_ref[rows, g * LANES:(g + 1) * LANES]
        cg = bc_ref[rows, (groups + g) * LANES:(groups + g + 1) * LANES]
        cbm = lax.dot_general(cg, bg, (((1,), (1,)), ((), ())),
                              preferred_element_type=F32).astype(BF16)
        bt = bg.T.reshape(SSM_STATE // BF16_SUBLANES, BF16_SUBLANES, q)
        st_g = jnp.concatenate([st_ref[g * pairs_g + p] for p in range(pairs_g)], axis=1)
        y_off = _dot(cg, st_g.astype(BF16))
        for p in range(pairs_g):
            slab = g * pairs_g + p
            lanes = slice(slab * LANES, (slab + 1) * LANES)
            lhs_y, lhs_s, a = [], [], []
            for h in (2 * slab, 2 * slab + 1):
                a_h = cs_ref[rows, h:h + 1]
                a.append(a_h)
                seg = a_h - rt_ref[k, h:h + 1, :]
                decay = jnp.exp2(jnp.where(causal, seg, -jnp.inf))
                lhs_y.append(cbm * decay.astype(BF16))
                w_h = jnp.broadcast_to(wt_ref[k, h:h + 1, :], (BF16_SUBLANES, q)).astype(BF16)
                lhs_s.append((bt * w_h[None]).reshape(SSM_STATE, q))
            lhs = jnp.concatenate([jnp.concatenate(lhs_y, axis=1),
                                   jnp.concatenate(lhs_s, axis=1)], axis=0)
            src = x0_ref if slab * LANES < half else x1_ref
            off = (slab * LANES) % half
            xs = src[rows, off:off + LANES].astype(F32)
            rhs = jnp.concatenate([jnp.where(left, xs, 0.0).astype(BF16),
                                   jnp.where(left, 0.0, xs).astype(BF16)], axis=0)
            res = _dot(lhs, rhs)
            e12 = jnp.exp2(jnp.where(left, a[0], a[1]))
            st_ref[slab] = st_ref[slab] * e12[q - 1:q, :] + res[q:, :]
            y = res[:q, :] + y_off[:, p * LANES:(p + 1) * LANES] * e12 + dsk_ref[:, lanes] * xs
            y_ref[rows, lanes] = y.astype(BF16)


def _ssd(xbc, cs, rt, wt, d_skip_e, *, batch, seq, chunks_per_step):
    n_tiles, m, tn = xbc.shape
    assert n_tiles == 3 and tn == 2 * SSM_GROUPS * SSM_STATE
    d_inner = 2 * tn
    q = SSM_CHUNK
    tr = chunks_per_step * q
    ns = seq // tr

    def tile(k):
        return pl.BlockSpec((None, tr, tn), lambda b, c: (k, b * ns + c, 0))

    def rows(width):
        return pl.BlockSpec((tr, width), lambda b, c: (b * ns + c, 0))

    per_chunk = pl.BlockSpec((chunks_per_step, LANES, q), lambda b, c: (b * ns + c, 0, 0))
    return pl.pallas_call(
        _ssd_kernel,
        grid=(batch, ns),
        in_specs=[tile(0), tile(1), tile(2), rows(LANES), per_chunk, per_chunk,
                  pl.BlockSpec(d_skip_e.shape, lambda b, c: (0, 0))],
        out_specs=rows(d_inner),
        out_shape=jax.ShapeDtypeStruct((m, d_inner), BF16),
        scratch_shapes=[pltpu.VMEM((d_inner // LANES, SSM_STATE, LANES), F32)],
        compiler_params=_params(("arbitrary", "arbitrary")),
        name="ssd",
    )(xbc, xbc, xbc, cs, rt, wt, d_skip_e)


def _mix_kernel(y_ref, z0_ref, z1_ref, gb_ref, ya_ref, ng_ref, wssm_ref, out_ref, yz_ref):
    gw = y_ref.shape[1] // SSM_GROUPS
    half = z0_ref.shape[1]
    for g in range(SSM_GROUPS):
        cols = slice(g * gw, (g + 1) * gw)
        src = z0_ref if g * gw < half else z1_ref
        off = (g * gw) % half
        yz = y_ref[:, cols].astype(F32) * src[:, off:off + gw].astype(F32)
        yz = yz * lax.rsqrt(jnp.mean(yz * yz, axis=-1, keepdims=True) + EPS)
        yz_ref[:, cols] = (yz * ng_ref[:, cols]).astype(BF16)

    yb = _dot(yz_ref[...], wssm_ref[...])
    out_ref[...] = (ya_ref[...].astype(F32) + _sigmoid(gb_ref[...].astype(F32)) * yb).astype(BF16)


def _mix(y, z_act, pa, ya, norm_g, w_ssm, *, tm, tile_gate_b):
    m, di = y.shape
    d = w_ssm.shape[1]
    assert z_act.shape == (2, m, di // 2) and pa.shape[2] == d

    def tile(width, k):
        return pl.BlockSpec((None, tm, width), lambda i: (k, i, 0))

    def rows(width):
        return pl.BlockSpec((tm, width), lambda i: (i, 0))

    return pl.pallas_call(
        _mix_kernel,
        grid=(m // tm,),
        in_specs=[rows(di), tile(di // 2, 0), tile(di // 2, 1), tile(d, tile_gate_b), rows(d),
                  _resident(norm_g.shape), _resident(w_ssm.shape)],
        out_specs=rows(d),
        out_shape=jax.ShapeDtypeStruct((m, d), BF16),
        scratch_shapes=[pltpu.VMEM((tm, di), BF16)],
        compiler_params=_params(("arbitrary",)),
        name="mix",
    )(y, z_act, z_act, pa, ya, norm_g, w_ssm)


def _out_proj_kernel(mixed_ref, x_ref, wo_ref, h_ref):
    h_ref[...] = x_ref[...] + _dot(mixed_ref[...], wo_ref[...])


def _out_proj(mixed, x2, w_o, *, tm):
    m, d = x2.shape

    def rows():
        return pl.BlockSpec((tm, d), lambda i: (i, 0))

    return pl.pallas_call(
        _out_proj_kernel,
        grid=(m // tm,),
        in_specs=[rows(), rows(), _resident(w_o.shape)],
        out_specs=rows(),
        out_shape=jax.ShapeDtypeStruct((m, d), F32),
        compiler_params=_params(("arbitrary",)),
        name="out_proj",
    )(mixed, x2, w_o)


def _ffn_kernel(h_ref, g_ref, wg_ref, wu_ref, wd_ref, gf_ref, out_ref, u_ref, *, final_norm):
    f = pl.program_id(1)

    @pl.when(f == 0)
    def _():
        h = h_ref[...]
        u_ref[...] = _rmsnorm(h, g_ref[...]).astype(BF16)
        out_ref[...] = h

    u = u_ref[...]
    tf = wg_ref.shape[1]
    part = None
    for lo in range(0, tf, tf // 2):
        cols = slice(lo, lo + tf // 2)
        gate = _dot(u, wg_ref[:, cols])
        act = (gate * _sigmoid(gate) * _dot(u, wu_ref[:, cols])).astype(BF16)
        p = _dot(act, wd_ref[cols, :])
        part = p if part is None else part + p
    out_ref[...] += part

    if final_norm:
        @pl.when(f == pl.num_programs(1) - 1)
        def _():
            out_ref[...] = _rmsnorm(out_ref[...], gf_ref[...])


def _ffn(h, g_ffn, w_gate, w_up, w_down, g_final, *, tm, tf, final_norm):
    m, d = h.shape
    ff = w_gate.shape[1]
    return pl.pallas_call(
        functools.partial(_ffn_kernel, final_norm=final_norm),
        grid=(m // tm, ff // tf),
        in_specs=[pl.BlockSpec((tm, d), lambda i, f: (i, 0)),
                  pl.BlockSpec((1, d), lambda i, f: (0, 0)),
                  pl.BlockSpec((d, tf), lambda i, f: (0, f)),
                  pl.BlockSpec((d, tf), lambda i, f: (0, f)),
                  pl.BlockSpec((tf, d), lambda i, f: (f, 0)),
                  pl.BlockSpec((1, d), lambda i, f: (0, 0))],
        out_specs=pl.BlockSpec((tm, d), lambda i, f: (i, 0)),
        out_shape=jax.ShapeDtypeStruct((m, d), F32),
        scratch_shapes=[pltpu.VMEM((tm, d), BF16)],
        compiler_params=_params(("arbitrary", "arbitrary")),
        name="ffn",
    )(h, g_ffn, w_gate, w_up, w_down, g_final)


def _pick(n, candidates):
    for c in candidates:
        if n % c == 0:
            return c
    raise ValueError(f"no tile in {candidates} divides {n}")


def kernel(x, norm_mix_g, w_in, conv_a_w, w_out_a, ssm_conv_w, ssm_conv_b, dt_bias, a_log, d_skip,
           ssm_norm_g, w_out_ssm, w_o, norm_ffn_g, w_ffn_gate, w_ffn_up, w_ffn_down, norm_final_g):
    batch, seq, d = x.shape
    m = batch * seq
    depth = w_in.shape[0]
    d_inner = w_out_ssm.shape[1]
    heads = dt_bias.shape[1]
    gn = SSM_GROUPS * SSM_STATE
    conv_dim = d_inner + 2 * gn
    n_main = w_in.shape[2] - heads
    off_a = 2 * d
    off_z = off_a + 3 * conv_a_w.shape[2]
    off_xbc = off_z + d_inner
    hp = d_inner // heads
    tn = d
    assert conv_a_w.shape[2] == d and off_xbc + conv_dim == n_main
    assert d_inner == 2 * tn and 2 * gn == tn
    assert seq % SSM_CHUNK == 0 and heads <= LANES and heads % (2 * SSM_GROUPS) == 0
    assert 2 * hp == LANES and SSM_STATE == LANES and SSM_CHUNK == LANES
    pad_h = ((0, 0), (0, LANES - heads))
    tm_proj = _pick(seq, (1024, 512, 256, 128))

    h = x.reshape(m, d)
    for i in range(depth):
        w_all = w_in[i].astype(BF16)
        w_dt = jnp.pad(w_in[i, :, n_main:], pad_h).astype(BF16)
        dtb = jnp.pad(dt_bias[i][None], pad_h)
        alog = jnp.pad(a_log[i][None], pad_h)
        d_skip_e = jnp.repeat(d_skip[i], hp)[None]

        pa, dt_raw, u = _norm_proj(h, norm_mix_g[i][None], w_all, w_dt, col0=0, n_tiles=off_z // tn,
                                   tm=tm_proj, tn=tn, epilogue="none")
        z_act = _proj(u, w_all, col0=off_z // tn, n_tiles=d_inner // tn,
                      tm=tm_proj, tn=tn, epilogue="silu")
        xbc = _proj_conv(u, w_all, ssm_conv_w[i], ssm_conv_b[i][None], col0=off_xbc // tn,
                         n_tiles=conv_dim // tn, seq=seq, tm=tm_proj, tn=tn)

        ya = _mixer_a(pa, conv_a_w[i], w_out_a[i].astype(BF16), seq=seq,
                      tm=_pick(seq, (512, 256, 128)), tile_gate=0, tile_b=off_a // tn)
        cs, rt, wt = _ssd_prep(dt_raw, dtb, alog, tr=tm_proj)
        y = _ssd(xbc, cs, rt, wt, d_skip_e, batch=batch, seq=seq,
                 chunks_per_step=_pick(seq // SSM_CHUNK, (SSD_CHUNKS_PER_STEP, 2, 1)))
        mixed = _mix(y, z_act, pa, ya, ssm_norm_g[i][None], w_out_ssm[i].astype(BF16),
                     tm=_pick(m, (512, 256, 128)), tile_gate_b=1)
        h = _out_proj(mixed, h, w_o[i].astype(BF16), tm=_pick(m, (1024, 512, 256, 128)))
        h = _ffn(h, norm_ffn_g[i][None], w_ffn_gate[i].astype(BF16), w_ffn_up[i].astype(BF16),
                 w_ffn_down[i].astype(BF16), norm_final_g[None],
                 tm=_pick(m, (1024, 512, 256, 128)), tf=_pick(w_ffn_gate.shape[2], (512, 256, 128)),
                 final_norm=(i == depth - 1))
    return h.reshape(batch, seq, d)
```

```python
import functools

import jax
import jax.numpy as jnp
from jax import lax
from jax.experimental import pallas as pl
from jax.experimental.pallas import tpu as pltpu

EPS = 1e-6
LOG2_E = 1.4426950408889634
SSM_GROUPS = 8
SSM_STATE = 128
SSM_CHUNK = 128
SSD_CHUNKS_PER_STEP = 4

LANES = 128
BF16_SUBLANES = 16
F32_SUBLANES = 8
V7X_VMEM_LIMIT_BYTES = 60000 * 1024

F32 = jnp.float32
BF16 = jnp.bfloat16


def _params(semantics):
    return pltpu.CompilerParams(dimension_semantics=semantics,
                                vmem_limit_bytes=V7X_VMEM_LIMIT_BYTES)


def _resident(shape):
    return pl.BlockSpec(shape, lambda *_: (0,) * len(shape), pipeline_mode=pl.Buffered(1))


def _rmsnorm(x, g):
    return x * lax.rsqrt(jnp.mean(x * x, axis=-1, keepdims=True) + EPS) * g


def _dot(a, b):
    return jnp.dot(a, b, preferred_element_type=F32)


def _sigmoid(x):
    return 1.0 / (1.0 + jnp.exp2(x * -LOG2_E))


def _store_projection(res, out_ref, epilogue):
    if epilogue == "silu":
        res = res * _sigmoid(res)
    else:
        assert epilogue == "none"
    out_ref[...] = res.astype(BF16)


def _tile_major(m, n_tiles, tm, tn):
    return (jax.ShapeDtypeStruct((n_tiles, m, tn), BF16),
            pl.BlockSpec((None, tm, tn), lambda i, j: (j, i, 0)))


def _norm_proj_kernel(x_ref, g_ref, w_ref, wdt_ref, out_ref, dt_ref, u_ref, *, epilogue):
    @pl.when(pl.program_id(1) == 0)
    def _():
        u = _rmsnorm(x_ref[...], g_ref[...]).astype(BF16)
        u_ref[...] = u
        dt_ref[...] = _dot(u, wdt_ref[...])

    _store_projection(_dot(u_ref[...], w_ref[...]), out_ref, epilogue)


def _norm_proj(x2, g, w_all, w_dt, *, col0, n_tiles, tm, tn, epilogue):
    m, d = x2.shape
    out_shape, out_spec = _tile_major(m, n_tiles, tm, tn)
    return pl.pallas_call(
        functools.partial(_norm_proj_kernel, epilogue=epilogue),
        grid=(m // tm, n_tiles),
        in_specs=[
            pl.BlockSpec((tm, d), lambda i, j: (i, 0)),
            pl.BlockSpec((1, d), lambda i, j: (0, 0)),
            pl.BlockSpec((d, tn), lambda i, j: (0, col0 + j)),
            pl.BlockSpec((d, LANES), lambda i, j: (0, 0)),
        ],
        out_specs=[
            out_spec,
            pl.BlockSpec((tm, LANES), lambda i, j: (i, 0)),
            pl.BlockSpec((tm, d), lambda i, j: (i, 0)),
        ],
        out_shape=[out_shape,
                   jax.ShapeDtypeStruct((m, LANES), F32),
                   jax.ShapeDtypeStruct((m, d), BF16)],
        compiler_params=_params(("arbitrary", "arbitrary")),
        name="norm_proj_" + epilogue,
    )(x2, g, w_all, w_dt)


def _proj_kernel(u_ref, w_ref, out_ref, *, epilogue):
    _store_projection(_dot(u_ref[...], w_ref[...]), out_ref, epilogue)


def _proj(u, w_all, *, col0, n_tiles, tm, tn, epilogue):
    m, d = u.shape
    out_shape, out_spec = _tile_major(m, n_tiles, tm, tn)
    return pl.pallas_call(
        functools.partial(_proj_kernel, epilogue=epilogue),
        grid=(m // tm, n_tiles),
        in_specs=[pl.BlockSpec((tm, d), lambda i, j: (i, 0)),
                  pl.BlockSpec((d, tn), lambda i, j: (0, col0 + j))],
        out_specs=out_spec,
        out_shape=out_shape,
        compiler_params=_params(("arbitrary", "arbitrary")),
        name="proj_" + epilogue,
    )(u, w_all)


def _proj_conv_kernel(u_ref, w_ref, cw_ref, cb_ref, out_ref, buf_ref, carry_ref, *, tiles_per_seq):
    tm = u_ref.shape[0]
    taps = cw_ref.shape[0]
    tail = F32_SUBLANES
    j = pl.program_id(1)
    seq_start = pl.program_id(0) % tiles_per_seq == 0
    res = _dot(u_ref[...], w_ref[...])
    for c in range(buf_ref.shape[0]):
        lanes = slice(c * LANES, (c + 1) * LANES)
        res_c = res[:, lanes]
        buf_ref[c, 0:tail, :] = jnp.where(seq_start, 0.0, carry_ref[j, c])
        buf_ref[c, tail:, :] = res_c
        acc = cb_ref[:, lanes] + cw_ref[taps - 1:taps, lanes] * res_c
        for k in range(taps - 1):
            lo = tail - (taps - 1 - k)
            acc = acc + cw_ref[k:k + 1, lanes] * buf_ref[c, lo:lo + tm, :]
        carry_ref[j, c] = buf_ref[c, tm:tm + tail, :]
        out_ref[:, lanes] = (acc * _sigmoid(acc)).astype(BF16)


def _proj_conv(u, w_all, conv_w, conv_b, *, col0, n_tiles, seq, tm, tn):
    m, d = u.shape
    taps = conv_w.shape[0]
    out_shape, out_spec = _tile_major(m, n_tiles, tm, tn)
    return pl.pallas_call(
        functools.partial(_proj_conv_kernel, tiles_per_seq=seq // tm),
        grid=(m // tm, n_tiles),
        in_specs=[pl.BlockSpec((tm, d), lambda i, j: (i, 0)),
                  pl.BlockSpec((d, tn), lambda i, j: (0, col0 + j)),
                  pl.BlockSpec((taps, tn), lambda i, j: (0, j)),
                  pl.BlockSpec((1, tn), lambda i, j: (0, j))],
        out_specs=out_spec,
        out_shape=out_shape,
        scratch_shapes=[pltpu.VMEM((tn // LANES, F32_SUBLANES + tm, LANES), F32),
                        pltpu.VMEM((n_tiles, tn // LANES, F32_SUBLANES, LANES), F32)],
        compiler_params=_params(("arbitrary", "arbitrary")),
        name="proj_conv",
    )(u, w_all, conv_w, conv_b)


MIXER_A_RING = 3


def _mixer_a_tile_copies(pa_hbm, in_ref, sem_ref, tiles, step, slot):
    tm = in_ref.shape[2]
    return [pltpu.make_async_copy(pa_hbm.at[t, pl.ds(step * tm, tm), :], in_ref.at[slot, k],
                                  sem_ref.at[slot, k])
            for k, t in enumerate(tiles)]


def _mixer_a_kernel(pa_hbm, cw_ref, wo_ref, out_ref,
                    in_ref, sem_ref, buf_ref, carry_ref, y_ref, *, tiles, tiles_per_seq):
    i = pl.program_id(0)
    n = pl.num_programs(0)
    ahead = MIXER_A_RING - 1

    @pl.when(i == 0)
    def _():
        for s in range(ahead):
            for cp in _mixer_a_tile_copies(pa_hbm, in_ref, sem_ref, tiles, s, s):
                cp.start()

    @pl.when(i + ahead < n)
    def _():
        for cp in _mixer_a_tile_copies(pa_hbm, in_ref, sem_ref, tiles, i + ahead,
                                       (i + ahead) % MIXER_A_RING):
            cp.start()

    slot = i % MIXER_A_RING
    for cp in _mixer_a_tile_copies(pa_hbm, in_ref, sem_ref, tiles, i, slot):
        cp.wait()
    ga_ref, b_ref, c_ref, h_ref = (in_ref.at[slot, k] for k in range(len(tiles)))

    tm = c_ref.shape[0]
    taps = cw_ref.shape[0]
    tail = F32_SUBLANES
    seq_start = i % tiles_per_seq == 0
    for s in range(buf_ref.shape[0]):
        lanes = slice(s * LANES, (s + 1) * LANES)
        v = c_ref[:, lanes].astype(F32) * h_ref[:, lanes].astype(F32)
        buf_ref[s, 0:tail, :] = jnp.where(seq_start, 0.0, carry_ref[s])
        buf_ref[s, tail:, :] = v
        acc = cw_ref[taps - 1:taps, lanes] * v
        for j in range(taps - 1):
            lo = tail - (taps - 1 - j)
            acc = acc + cw_ref[j:j + 1, lanes] * buf_ref[s, lo:lo + tm, :]
        carry_ref[s] = buf_ref[s, tm:tm + tail, :]
        y_ref[:, lanes] = (b_ref[:, lanes].astype(F32) * acc).astype(BF16)
    out_ref[...] = (_sigmoid(ga_ref[...].astype(F32)) * _dot(y_ref[...], wo_ref[...])).astype(BF16)


def _mixer_a(pa, conv_w, w_out, *, seq, tm, tile_gate, tile_b):
    m = pa.shape[1]
    c = conv_w.shape[1]
    d = w_out.shape[1]
    tiles = (tile_gate, tile_b, tile_b + 1, tile_b + 2)
    assert m // tm >= MIXER_A_RING and pa.shape[2] == c

    return pl.pallas_call(
        functools.partial(_mixer_a_kernel, tiles=tiles, tiles_per_seq=seq // tm),
        grid=(m // tm,),
        in_specs=[pl.BlockSpec(memory_space=pl.ANY),
                  _resident(conv_w.shape), _resident(w_out.shape)],
        out_specs=pl.BlockSpec((tm, d), lambda i: (i, 0)),
        out_shape=jax.ShapeDtypeStruct((m, d), BF16),
        scratch_shapes=[pltpu.VMEM((MIXER_A_RING, len(tiles), tm, c), BF16),
                        pltpu.SemaphoreType.DMA((MIXER_A_RING, len(tiles))),
                        pltpu.VMEM((c // LANES, F32_SUBLANES + tm, LANES), F32),
                        pltpu.VMEM((c // LANES, F32_SUBLANES, LANES), F32),
                        pltpu.VMEM((tm, c), BF16)],
        compiler_params=_params(("arbitrary",)),
        name="mixer_a",
    )(pa, conv_w, w_out)


def _ssd_prep_kernel(dt_ref, dtb_ref, alog_ref, cs_ref, rt_ref, wt_ref):
    q = SSM_CHUNK
    row = lax.broadcasted_iota(jnp.int32, (q, q), 0)
    col = lax.broadcasted_iota(jnp.int32, (q, q), 1)
    tril = (col <= row).astype(F32)
    a = -jnp.exp(alog_ref[...])
    for k in range(rt_ref.shape[0]):
        rows = slice(k * q, (k + 1) * q)
        dt = jax.nn.softplus(dt_ref[rows, :] + dtb_ref[...])
        cs = LOG2_E * jnp.dot(tril, dt * a, preferred_element_type=F32,
                              precision=lax.Precision.HIGHEST)
        r = cs - LOG2_E * jnp.log(dt)
        cs_ref[rows, :] = cs
        rt_ref[k] = r.T
        wt_ref[k] = jnp.exp2(cs[q - 1:q, :] - r).T


def _ssd_prep(dt_raw, dt_bias, a_log, *, tr):
    m = dt_raw.shape[0]
    q = SSM_CHUNK
    per_chunk = jax.ShapeDtypeStruct((m // q, LANES, q), F32)
    return pl.pallas_call(
        _ssd_prep_kernel,
        grid=(m // tr,),
        in_specs=[pl.BlockSpec((tr, LANES), lambda i: (i, 0)),
                  pl.BlockSpec(dt_bias.shape, lambda i: (0, 0)),
                  pl.BlockSpec(a_log.shape, lambda i: (0, 0))],
        out_specs=[pl.BlockSpec((tr, LANES), lambda i: (i, 0)),
                   pl.BlockSpec((tr // q, LANES, q), lambda i: (i, 0, 0)),
                   pl.BlockSpec((tr // q, LANES, q), lambda i: (i, 0, 0))],
        out_shape=[jax.ShapeDtypeStruct((m, LANES), F32), per_chunk, per_chunk],
        compiler_params=_params(("arbitrary",)),
        name="ssd_prep",
    )(dt_raw, dt_bias, a_log)


def _ssd_kernel(x0_ref, x1_ref, bc_ref, cs_ref, rt_ref, wt_ref, dsk_ref, y_ref, st_ref):
    q = SSM_CHUNK
    hp = LANES // 2

    @pl.when(pl.program_id(1) == 0)
    def _():
        st_ref[...] = jnp.zeros(st_ref.shape, F32)

    row = lax.broadcasted_iota(jnp.int32, (q, q), 0)
    col = lax.broadcasted_iota(jnp.int32, (q, q), 1)
    causal = col <= row
    left = lax.broadcasted_iota(jnp.int32, (q, LANES), 1) < hp
    for k in range(rt_ref.shape[0]):
        _ssd_chunk(k, x0_ref, x1_ref, bc_ref, cs_ref, rt_ref, wt_ref, dsk_ref, y_ref, st_ref,
                   causal, left)


def _ssd_chunk(k, x0_ref, x1_ref, bc_ref, cs_ref, rt_ref, wt_ref, dsk_ref, y_ref, st_ref, causal, left):
    q = SSM_CHUNK
    groups = SSM_GROUPS
    pairs_g = st_ref.shape[0] // groups
    half = x0_ref.shape[1]
    rows = slice(k * q, (k + 1) * q)
    for g in range(groups):
        bg = bc_ref[rows, g * LANES:(g + 1) * LANES]
        cg = bc_ref[rows, (groups + g) * LANES:(groups + g + 1) * LANES]
        cbm = lax.dot_general(cg, bg, (((1,), (1,)), ((), ())),
                              preferred_element_type=F32).astype(BF16)
        bt = bg.T.reshape(SSM_STATE // BF16_SUBLANES, BF16_SUBLANES, q)
        st_g = jnp.concatenate([st_ref[g * pairs_g + p] for p in range(pairs_g)], axis=1)
        y_off = _dot(cg, st_g.astype(BF16))
        for p in range(pairs_g):
            slab = g * pairs_g + p
            lanes = slice(slab * LANES, (slab + 1) * LANES)
            lhs_y, lhs_s, a = [], [], []
            for h in (2 * slab, 2 * slab + 1):
                a_h = cs_ref[rows, h:h + 1]
                a.append(a_h)
                seg = a_h - rt_ref[k, h:h + 1, :]
                decay = jnp.exp2(jnp.where(causal, seg, -jnp.inf))
                lhs_y.append(cbm * decay.astype(BF16))
                w_h = jnp.broadcast_to(wt_ref[k, h:h + 1, :], (BF16_SUBLANES, q)).astype(BF16)
                lhs_s.append((bt * w_h[None]).reshape(SSM_STATE, q))
            lhs = jnp.concatenate([jnp.concatenate(lhs_y, axis=1),
                                   jnp.concatenate(lhs_s, axis=1)], axis=0)
            src = x0_ref if slab * LANES < half else x1_ref
            off = (slab * LANES) % half
            xs = src[rows, off:off + LANES].astype(F32)
            rhs = jnp.concatenate([jnp.where(left, xs, 0.0).astype(BF16),
                                   jnp.where(left, 0.0, xs).astype(BF16)], axis=0)
            res = _dot(lhs, rhs)
            e12 = jnp.exp2(jnp.where(left, a[0], a[1]))
            st_ref[slab] = st_ref[slab] * e12[q - 1:q, :] + res[q:, :]
            y = res[:q, :] + y_off[:, p * LANES:(p + 1) * LANES] * e12 + dsk_ref[:, lanes] * xs
            y_ref[rows, lanes] = y.astype(BF16)


def _ssd(xbc, cs, rt, wt, d_skip_e, *, batch, seq, chunks_per_step):
    n_tiles, m, tn = xbc.shape
    assert n_tiles == 3 and tn == 2 * SSM_GROUPS * SSM_STATE
    d_inner = 2 * tn
    q = SSM_CHUNK
    tr = chunks_per_step * q
    ns = seq // tr

    def tile(k):
        return pl.BlockSpec((None, tr, tn), lambda b, c: (k, b * ns + c, 0))

    def rows(width):
        return pl.BlockSpec((tr, width), lambda b, c: (b * ns + c, 0))

    per_chunk = pl.BlockSpec((chunks_per_step, LANES, q), lambda b, c: (b * ns + c, 0, 0))
    return pl.pallas_call(
        _ssd_kernel,
        grid=(batch, ns),
        in_specs=[tile(0), tile(1), tile(2), rows(LANES), per_chunk, per_chunk,
                  pl.BlockSpec(d_skip_e.shape, lambda b, c: (0, 0))],
        out_specs=rows(d_inner),
        out_shape=jax.ShapeDtypeStruct((m, d_inner), BF16),
        scratch_shapes=[pltpu.VMEM((d_inner // LANES, SSM_STATE, LANES), F32)],
        compiler_params=_params(("arbitrary", "arbitrary")),
        name="ssd",
    )(xbc, xbc, xbc, cs, rt, wt, d_skip_e)


def _mix_kernel(y_ref, z0_ref, z1_ref, gb_ref, ya_ref, ng_ref, wssm_ref, out_ref, yz_ref):
    gw = y_ref.shape[1] // SSM_GROUPS
    half = z0_ref.shape[1]
    for g in range(SSM_GROUPS):
        cols = slice(g * gw, (g + 1) * gw)
        src = z0_ref if g * gw < half else z1_ref
        off = (g * gw) % half
        yz = y_ref[:, cols].astype(F32) * src[:, off:off + gw].astype(F32)
        yz = yz * lax.rsqrt(jnp.mean(yz * yz, axis=-1, keepdims=True) + EPS)
        yz_ref[:, cols] = (yz * ng_ref[:, cols]).astype(BF16)

    yb = _dot(yz_ref[...], wssm_ref[...])
    out_ref[...] = (ya_ref[...].astype(F32) + _sigmoid(gb_ref[...].astype(F32)) * yb).astype(BF16)


def _mix(y, z_act, pa, ya, norm_g, w_ssm, *, tm, tile_gate_b):
    m, di = y.shape
    d = w_ssm.shape[1]
    assert z_act.shape == (2, m, di // 2) and pa.shape[2] == d

    def tile(width, k):
        return pl.BlockSpec((None, tm, width), lambda i: (k, i, 0))

    def rows(width):
        return pl.BlockSpec((tm, width), lambda i: (i, 0))

    return pl.pallas_call(
        _mix_kernel,
        grid=(m // tm,),
        in_specs=[rows(di), tile(di // 2, 0), tile(di // 2, 1), tile(d, tile_gate_b), rows(d),
                  _resident(norm_g.shape), _resident(w_ssm.shape)],
        out_specs=rows(d),
        out_shape=jax.ShapeDtypeStruct((m, d), BF16),
        scratch_shapes=[pltpu.VMEM((tm, di), BF16)],
        compiler_params=_params(("arbitrary",)),
        name="mix",
    )(y, z_act, z_act, pa, ya, norm_g, w_ssm)


def _out_proj_kernel(mixed_ref, x_ref, wo_ref, h_ref):
    h_ref[...] = x_ref[...] + _dot(mixed_ref[...], wo_ref[...])


def _out_proj(mixed, x2, w_o, *, tm):
    m, d = x2.shape

    def rows():
        return pl.BlockSpec((tm, d), lambda i: (i, 0))

    return pl.pallas_call(
        _out_proj_kernel,
        grid=(m // tm,),
        in_specs=[rows(), rows(), _resident(w_o.shape)],
        out_specs=rows(),
        out_shape=jax.ShapeDtypeStruct((m, d), F32),
        compiler_params=_params(("arbitrary",)),
        name="out_proj",
    )(mixed, x2, w_o)


def _ffn_kernel(h_ref, g_ref, wg_ref, wu_ref, wd_ref, gf_ref, out_ref, u_ref, *, final_norm):
    f = pl.program_id(1)

    @pl.when(f == 0)
    def _():
        h = h_ref[...]
        u_ref[...] = _rmsnorm(h, g_ref[...]).astype(BF16)
        out_ref[...] = h

    u = u_ref[...]
    tf = wg_ref.shape[1]
    part = None
    for lo in range(0, tf, tf // 2):
        cols = slice(lo, lo + tf // 2)
        gate = _dot(u, wg_ref[:, cols])
        act = (gate * _sigmoid(gate) * _dot(u, wu_ref[:, cols])).astype(BF16)
        p = _dot(act, wd_ref[cols, :])
        part = p if part is None else part + p
    out_ref[...] += part

    if final_norm:
        @pl.when(f == pl.num_programs(1) - 1)
        def _():
            out_ref[...] = _rmsnorm(out_ref[...], gf_ref[...])


def _ffn(h, g_ffn, w_gate, w_up, w_down, g_final, *, tm, tf, final_norm):
    m, d = h.shape
    ff = w_gate.shape[1]
    return pl.pallas_call(
        functools.partial(_ffn_kernel, final_norm=final_norm),
        grid=(m // tm, ff // tf),
        in_specs=[pl.BlockSpec((tm, d), lambda i, f: (i, 0)),
                  pl.BlockSpec((1, d), lambda i, f: (0, 0)),
                  pl.BlockSpec((d, tf), lambda i, f: (0, f)),
                  pl.BlockSpec((d, tf), lambda i, f: (0, f)),
                  pl.BlockSpec((tf, d), lambda i, f: (f, 0)),
                  pl.BlockSpec((1, d), lambda i, f: (0, 0))],
        out_specs=pl.BlockSpec((tm, d), lambda i, f: (i, 0)),
        out_shape=jax.ShapeDtypeStruct((m, d), F32),
        scratch_shapes=[pltpu.VMEM((tm, d), BF16)],
        compiler_params=_params(("arbitrary", "arbitrary")),
        name="ffn",
    )(h, g_ffn, w_gate, w_up, w_down, g_final)


def _pick(n, candidates):
    for c in candidates:
        if n % c == 0:
            return c
    raise ValueError(f"no tile in {candidates} divides {n}")


def kernel(x, norm_mix_g, w_in, conv_a_w, w_out_a, ssm_conv_w, ssm_conv_b, dt_bias, a_log, d_skip,
           ssm_norm_g, w_out_ssm, w_o, norm_ffn_g, w_ffn_gate, w_ffn_up, w_ffn_down, norm_final_g):
    batch, seq, d = x.shape
    m = batch * seq
    depth = w_in.shape[0]
    d_inner = w_out_ssm.shape[1]
    heads = dt_bias.shape[1]
    gn = SSM_GROUPS * SSM_STATE
    conv_dim = d_inner + 2 * gn
    n_main = w_in.shape[2] - heads
    off_a = 2 * d
    off_z = off_a + 3 * conv_a_w.shape[2]
    off_xbc = off_z + d_inner
    hp = d_inner // heads
    tn = d
    assert conv_a_w.shape[2] == d and off_xbc + conv_dim == n_main
    assert d_inner == 2 * tn and 2 * gn == tn
    assert seq % SSM_CHUNK == 0 and heads <= LANES and heads % (2 * SSM_GROUPS) == 0
    assert 2 * hp == LANES and SSM_STATE == LANES and SSM_CHUNK == LANES
    pad_h = ((0, 0), (0, LANES - heads))
    tm_proj = _pick(seq, (1024, 512, 256, 128))

    h = x.reshape(m, d)
    for i in range(depth):
        w_all = w_in[i].astype(BF16)
        w_dt = jnp.pad(w_in[i, :, n_main:], pad_h).astype(BF16)
        dtb = jnp.pad(dt_bias[i][None], pad_h)
        alog = jnp.pad(a_log[i][None], pad_h)
        d_skip_e = jnp.repeat(d_skip[i], hp)[None]

        pa, dt_raw, u = _norm_proj(h, norm_mix_g[i][None], w_all, w_dt, col0=0, n_tiles=off_z // tn,
                                   tm=tm_proj, tn=tn, epilogue="none")
        z_act = _proj(u, w_all, col0=off_z // tn, n_tiles=d_inner // tn,
                      tm=tm_proj, tn=tn, epilogue="silu")
        xbc = _proj_conv(u, w_all, ssm_conv_w[i], ssm_conv_b[i][None], col0=off_xbc // tn,
                         n_tiles=conv_dim // tn, seq=seq, tm=tm_proj, tn=tn)

        ya = _mixer_a(pa, conv_a_w[i], w_out_a[i].astype(BF16), seq=seq,
                      tm=_pick(seq, (512, 256, 128)), tile_gate=0, tile_b=off_a // tn)
        cs, rt, wt = _ssd_prep(dt_raw, dtb, alog, tr=tm_proj)
        y = _ssd(xbc, cs, rt, wt, d_skip_e, batch=batch, seq=seq,
                 chunks_per_step=_pick(seq // SSM_CHUNK, (SSD_CHUNKS_PER_STEP, 2, 1)))
        mixed = _mix(y, z_act, pa, ya, ssm_norm_g[i][None], w_out_ssm[i].astype(BF16),
                     tm=_pick(m, (512, 256, 128)), tile_gate_b=1)
        h = _out_proj(mixed, h, w_o[i].astype(BF16), tm=_pick(m, (1024, 512, 256, 128)))
        h = _ffn(h, norm_ffn_g[i][None], w_ffn_gate[i].astype(BF16), w_ffn_up[i].astype(BF16),
                 w_ffn_down[i].astype(BF16), norm_final_g[None],
                 tm=_pick(m, (1024, 512, 256, 128)), tf=_pick(w_ffn_gate.shape[2], (512, 256, 128)),
                 final_norm=(i == depth - 1))
    return h.reshape(batch, seq, d)
```

```python
import functools

import jax
import jax.numpy as jnp
from jax import lax
from jax.experimental import pallas as pl
from jax.experimental.pallas import tpu as pltpu

EPS = 1e-6
LOG2_E = 1.4426950408889634
SSM_GROUPS = 8
SSM_STATE = 128
SSM_CHUNK = 128
SSD_CHUNKS_PER_STEP = 4

LANES = 128
BF16_SUBLANES = 16
F32_SUBLANES = 8
V7X_VMEM_LIMIT_BYTES = 60000 * 1024

F32 = jnp.float32
BF16 = jnp.bfloat16


def _params(semantics):
    return pltpu.CompilerParams(dimension_semantics=semantics,
                                vmem_limit_bytes=V7X_VMEM_LIMIT_BYTES)


def _resident(shape):
    return pl.BlockSpec(shape, lambda *_: (0,) * len(shape), pipeline_mode=pl.Buffered(1))


def _rmsnorm(x, g):
    return x * lax.rsqrt(jnp.mean(x * x, axis=-1, keepdims=True) + EPS) * g


def _dot(a, b):
    return jnp.dot(a, b, preferred_element_type=F32)


def _sigmoid(x):
    return 1.0 / (1.0 + jnp.exp2(x * -LOG2_E))


def _store_projection(res, out_ref, epilogue):
    if epilogue == "silu":
        res = res * _sigmoid(res)
    else:
        assert epilogue == "none"
    out_ref[...] = res.astype(BF16)


def _tile_major(m, n_tiles, tm, tn):
    return (jax.ShapeDtypeStruct((n_tiles, m, tn), BF16),
            pl.BlockSpec((None, tm, tn), lambda i, j: (j, i, 0)))


def _norm_proj_kernel(x_ref, g_ref, w_ref, wdt_ref, out_ref, dt_ref, u_ref, *, epilogue):
    @pl.when(pl.program_id(1) == 0)
    def _():
        u = _rmsnorm(x_ref[...], g_ref[...]).astype(BF16)
        u_ref[...] = u
        dt_ref[...] = _dot(u, wdt_ref[...])

    _store_projection(_dot(u_ref[...], w_ref[...]), out_ref, epilogue)


def _norm_proj(x2, g, w_all, w_dt, *, col0, n_tiles, tm, tn, epilogue):
    m, d = x2.shape
    out_shape, out_spec = _tile_major(m, n_tiles, tm, tn)
    return pl.pallas_call(
        functools.partial(_norm_proj_kernel, epilogue=epilogue),
        grid=(m // tm, n_tiles),
        in_specs=[
            pl.BlockSpec((tm, d), lambda i, j: (i, 0)),
            pl.BlockSpec((1, d), lambda i, j: (0, 0)),
            pl.BlockSpec((d, tn), lambda i, j: (0, col0 + j)),
            pl.BlockSpec((d, LANES), lambda i, j: (0, 0)),
        ],
        out_specs=[
            out_spec,
            pl.BlockSpec((tm, LANES), lambda i, j: (i, 0)),
            pl.BlockSpec((tm, d), lambda i, j: (i, 0)),
        ],
        out_shape=[out_shape,
                   jax.ShapeDtypeStruct((m, LANES), F32),
                   jax.ShapeDtypeStruct((m, d), BF16)],
        compiler_params=_params(("arbitrary", "arbitrary")),
        name="norm_proj_" + epilogue,
    )(x2, g, w_all, w_dt)


def _proj_kernel(u_ref, w_ref, out_ref, *, epilogue):
    _store_projection(_dot(u_ref[...], w_ref[...]), out_ref, epilogue)


def _proj(u, w_all, *, col0, n_tiles, tm, tn, epilogue):
    m, d = u.shape
    out_shape, out_spec = _tile_major(m, n_tiles, tm, tn)
    return pl.pallas_call(
        functools.partial(_proj_kernel, epilogue=epilogue),
        grid=(m // tm, n_tiles),
        in_specs=[pl.BlockSpec((tm, d), lambda i, j: (i, 0)),
                  pl.BlockSpec((d, tn), lambda i, j: (0, col0 + j))],
        out_specs=out_spec,
        out_shape=out_shape,
        compiler_params=_params(("arbitrary", "arbitrary")),
        name="proj_" + epilogue,
    )(u, w_all)


def _proj_conv_kernel(u_ref, w_ref, cw_ref, cb_ref, out_ref, buf_ref, carry_ref, *, tiles_per_seq):
    tm = u_ref.shape[0]
    taps = cw_ref.shape[0]
    tail = F32_SUBLANES
    j = pl.program_id(1)
    seq_start = pl.program_id(0) % tiles_per_seq == 0
    res = _dot(u_ref[...], w_ref[...])
    for c in range(buf_ref.shape[0]):
        lanes = slice(c * LANES, (c + 1) * LANES)
        res_c = res[:, lanes]
        buf_ref[c, 0:tail, :] = jnp.where(seq_start, 0.0, carry_ref[j, c])
        buf_ref[c, tail:, :] = res_c
        acc = cb_ref[:, lanes] + cw_ref[taps - 1:taps, lanes] * res_c
        for k in range(taps - 1):
            lo = tail - (taps - 1 - k)
            acc = acc + cw_ref[k:k + 1, lanes] * buf_ref[c, lo:lo + tm, :]
        carry_ref[j, c] = buf_ref[c, tm:tm + tail, :]
        out_ref[:, lanes] = (acc * _sigmoid(acc)).astype(BF16)


def _proj_conv(u, w_all, conv_w, conv_b, *, col0, n_tiles, seq, tm, tn):
    m, d = u.shape
    taps = conv_w.shape[0]
    out_shape, out_spec = _tile_major(m, n_tiles, tm, tn)
    return pl.pallas_call(
        functools.partial(_proj_conv_kernel, tiles_per_seq=seq // tm),
        grid=(m // tm, n_tiles),
        in_specs=[pl.BlockSpec((tm, d), lambda i, j: (i, 0)),
                  pl.BlockSpec((d, tn), lambda i, j: (0, col0 + j)),
                  pl.BlockSpec((taps, tn), lambda i, j: (0, j)),
                  pl.BlockSpec((1, tn), lambda i, j: (0, j))],
        out_specs=out_spec,
        out_shape=out_shape,
        scratch_shapes=[pltpu.VMEM((tn // LANES, F32_SUBLANES + tm, LANES), F32),
                        pltpu.VMEM((n_tiles, tn // LANES, F32_SUBLANES, LANES), F32)],
        compiler_params=_params(("arbitrary", "arbitrary")),
        name="proj_conv",
    )(u, w_all, conv_w, conv_b)


def _mixer_a_kernel(ga_ref, b_ref, c_ref, h_ref, cw_ref, wo_ref, out_ref,
                    buf_ref, carry_ref, y_ref, *, tiles_per_seq):
    tm = c_ref.shape[0]
    taps = cw_ref.shape[0]
    tail = F32_SUBLANES
    seq_start = pl.program_id(0) % tiles_per_seq == 0
    for s in range(buf_ref.shape[0]):
        lanes = slice(s * LANES, (s + 1) * LANES)
        v = c_ref[:, lanes].astype(F32) * h_ref[:, lanes].astype(F32)
        buf_ref[s, 0:tail, :] = jnp.where(seq_start, 0.0, carry_ref[s])
        buf_ref[s, tail:, :] = v
        acc = cw_ref[taps - 1:taps, lanes] * v
        for j in range(taps - 1):
            lo = tail - (taps - 1 - j)
            acc = acc + cw_ref[j:j + 1, lanes] * buf_ref[s, lo:lo + tm, :]
        carry_ref[s] = buf_ref[s, tm:tm + tail, :]
        y_ref[:, lanes] = (b_ref[:, lanes].astype(F32) * acc).astype(BF16)
    out_ref[...] = (_sigmoid(ga_ref[...].astype(F32)) * _dot(y_ref[...], wo_ref[...])).astype(BF16)


def _mixer_a(pa, conv_w, w_out, *, seq, tm, tile_gate, tile_b):
    m = pa.shape[1]
    c = conv_w.shape[1]
    d = w_out.shape[1]

    def col(k):
        return pl.BlockSpec((None, tm, c), lambda i: (k, i, 0))

    return pl.pallas_call(
        functools.partial(_mixer_a_kernel, tiles_per_seq=seq // tm),
        grid=(m // tm,),
        in_specs=[col(tile_gate), col(tile_b), col(tile_b + 1), col(tile_b + 2),
                  _resident(conv_w.shape), _resident(w_out.shape)],
        out_specs=pl.BlockSpec((tm, d), lambda i: (i, 0)),
        out_shape=jax.ShapeDtypeStruct((m, d), BF16),
        scratch_shapes=[pltpu.VMEM((c // LANES, F32_SUBLANES + tm, LANES), F32),
                        pltpu.VMEM((c // LANES, F32_SUBLANES, LANES), F32),
                        pltpu.VMEM((tm, c), BF16)],
        compiler_params=_params(("arbitrary",)),
        name="mixer_a",
    )(pa, pa, pa, pa, conv_w, w_out)


def _ssd_prep_kernel(dt_ref, dtb_ref, alog_ref, cs_ref, rt_ref, wt_ref):
    q = SSM_CHUNK
    row = lax.broadcasted_iota(jnp.int32, (q, q), 0)
    col = lax.broadcasted_iota(jnp.int32, (q, q), 1)
    tril = (col <= row).astype(F32)
    a = -jnp.exp(alog_ref[...])
    for k in range(rt_ref.shape[0]):
        rows = slice(k * q, (k + 1) * q)
        dt = jax.nn.softplus(dt_ref[rows, :] + dtb_ref[...])
        cs = LOG2_E * jnp.dot(tril, dt * a, preferred_element_type=F32,
                              precision=lax.Precision.HIGHEST)
        r = cs - LOG2_E * jnp.log(dt)
        cs_ref[rows, :] = cs
        rt_ref[k] = r.T
        wt_ref[k] = jnp.exp2(cs[q - 1:q, :] - r).T


def _ssd_prep(dt_raw, dt_bias, a_log, *, tr):
    m = dt_raw.shape[0]
    q = SSM_CHUNK
    per_chunk = jax.ShapeDtypeStruct((m // q, LANES, q), F32)
    return pl.pallas_call(
        _ssd_prep_kernel,
        grid=(m // tr,),
        in_specs=[pl.BlockSpec((tr, LANES), lambda i: (i, 0)),
                  pl.BlockSpec(dt_bias.shape, lambda i: (0, 0)),
                  pl.BlockSpec(a_log.shape, lambda i: (0, 0))],
        out_specs=[pl.BlockSpec((tr, LANES), lambda i: (i, 0)),
                   pl.BlockSpec((tr // q, LANES, q), lambda i: (i, 0, 0)),
                   pl.BlockSpec((tr // q, LANES, q), lambda i: (i, 0, 0))],
        out_shape=[jax.ShapeDtypeStruct((m, LANES), F32), per_chunk, per_chunk],
        compiler_params=_params(("arbitrary",)),
        name="ssd_prep",
    )(dt_raw, dt_bias, a_log)


def _ssd_kernel(x0_ref, x1_ref, bc_ref, cs_ref, rt_ref, wt_ref, dsk_ref, y_ref, st_ref):
    q = SSM_CHUNK
    hp = LANES // 2

    @pl.when(pl.program_id(1) == 0)
    def _():
        st_ref[...] = jnp.zeros(st_ref.shape, F32)

    row = lax.broadcasted_iota(jnp.int32, (q, q), 0)
    col = lax.broadcasted_iota(jnp.int32, (q, q), 1)
    causal = col <= row
    left = lax.broadcasted_iota(jnp.int32, (q, LANES), 1) < hp
    for k in range(rt_ref.shape[0]):
        _ssd_chunk(k, x0_ref, x1_ref, bc_ref, cs_ref, rt_ref, wt_ref, dsk_ref, y_ref, st_ref,
                   causal, left)


def _ssd_chunk(k, x0_ref, x1_ref, bc_ref, cs_ref, rt_ref, wt_ref, dsk_ref, y_ref, st_ref, causal, left):
    q = SSM_CHUNK
    groups = SSM_GROUPS
    pairs_g = st_ref.shape[0] // groups
    half = x0_ref.shape[1]
    rows = slice(k * q, (k + 1) * q)
    for g in range(groups):
        bg = bc_ref[rows, g * LANES:(g + 1) * LANES]
        cg = bc_ref[rows, (groups + g) * LANES:(groups + g + 1) * LANES]
        cbm = lax.dot_general(cg, bg, (((1,), (1,)), ((), ())),
                              preferred_element_type=F32).astype(BF16)
        bt = bg.T.reshape(SSM_STATE // BF16_SUBLANES, BF16_SUBLANES, q)
        st_g = jnp.concatenate([st_ref[g * pairs_g + p] for p in range(pairs_g)], axis=1)
        y_off = _dot(cg, st_g.astype(BF16))
        for p in range(pairs_g):
            slab = g * pairs_g + p
            lanes = slice(slab * LANES, (slab + 1) * LANES)
            lhs_y, lhs_s, a = [], [], []
            for h in (2 * slab, 2 * slab + 1):
                a_h = cs_ref[rows, h:h + 1]
                a.append(a_h)
                seg = a_h - rt_ref[k, h:h + 1, :]
                decay = jnp.exp2(jnp.where(causal, seg, -jnp.inf))
                lhs_y.append(cbm * decay.astype(BF16))
                w_h = jnp.broadcast_to(wt_ref[k, h:h + 1, :], (BF16_SUBLANES, q)).astype(BF16)
                lhs_s.append((bt * w_h[None]).reshape(SSM_STATE, q))
            lhs = jnp.concatenate([jnp.concatenate(lhs_y, axis=1),
                                   jnp.concatenate(lhs_s, axis=1)], axis=0)
            src = x0_ref if slab * LANES < half else x1_ref
            off = (slab * LANES) % half
            xs = src[rows, off:off + LANES].astype(F32)
            rhs = jnp.concatenate([jnp.where(left, xs, 0.0).astype(BF16),
                                   jnp.where(left, 0.0, xs).astype(BF16)], axis=0)
            res = _dot(lhs, rhs)
            e12 = jnp.exp2(jnp.where(left, a[0], a[1]))
            st_ref[slab] = st_ref[slab] * e12[q - 1:q, :] + res[q:, :]
            y = res[:q, :] + y_off[:, p * LANES:(p + 1) * LANES] * e12 + dsk_ref[:, lanes] * xs
            y_ref[rows, lanes] = y.astype(BF16)


def _ssd(xbc, cs, rt, wt, d_skip_e, *, batch, seq, chunks_per_step):
    n_tiles, m, tn = xbc.shape
    assert n_tiles == 3 and tn == 2 * SSM_GROUPS * SSM_STATE
    d_inner = 2 * tn
    q = SSM_CHUNK
    tr = chunks_per_step * q
    ns = seq // tr

    def tile(k):
        return pl.BlockSpec((None, tr, tn), lambda b, c: (k, b * ns + c, 0))

    def rows(width):
        return pl.BlockSpec((tr, width), lambda b, c: (b * ns + c, 0))

    per_chunk = pl.BlockSpec((chunks_per_step, LANES, q), lambda b, c: (b * ns + c, 0, 0))
    return pl.pallas_call(
        _ssd_kernel,
        grid=(batch, ns),
        in_specs=[tile(0), tile(1), tile(2), rows(LANES), per_chunk, per_chunk,
                  pl.BlockSpec(d_skip_e.shape, lambda b, c: (0, 0))],
        out_specs=rows(d_inner),
        out_shape=jax.ShapeDtypeStruct((m, d_inner), BF16),
        scratch_shapes=[pltpu.VMEM((d_inner // LANES, SSM_STATE, LANES), F32)],
        compiler_params=_params(("arbitrary", "arbitrary")),
        name="ssd",
    )(xbc, xbc, xbc, cs, rt, wt, d_skip_e)


def _mix_kernel(y_ref, z0_ref, z1_ref, gb_ref, ya_ref, ng_ref, wssm_ref, out_ref, yz_ref):
    gw = y_ref.shape[1] // SSM_GROUPS
    half = z0_ref.shape[1]
    for g in range(SSM_GROUPS):
        cols = slice(g * gw, (g + 1) * gw)
        src = z0_ref if g * gw < half else z1_ref
        off = (g * gw) % half
        yz = y_ref[:, cols].astype(F32) * src[:, off:off + gw].astype(F32)
        yz = yz * lax.rsqrt(jnp.mean(yz * yz, axis=-1, keepdims=True) + EPS)
        yz_ref[:, cols] = (yz * ng_ref[:, cols]).astype(BF16)

    yb = _dot(yz_ref[...], wssm_ref[...])
    out_ref[...] = (ya_ref[...].astype(F32) + _sigmoid(gb_ref[...].astype(F32)) * yb).astype(BF16)


def _mix(y, z_act, pa, ya, norm_g, w_ssm, *, tm, tile_gate_b):
    m, di = y.shape
    d = w_ssm.shape[1]
    assert z_act.shape == (2, m, di // 2) and pa.shape[2] == d

    def tile(width, k):
        return pl.BlockSpec((None, tm, width), lambda i: (k, i, 0))

    def rows(width):
        return pl.BlockSpec((tm, width), lambda i: (i, 0))

    return pl.pallas_call(
        _mix_kernel,
        grid=(m // tm,),
        in_specs=[rows(di), tile(di // 2, 0), tile(di // 2, 1), tile(d, tile_gate_b), rows(d),
                  _resident(norm_g.shape), _resident(w_ssm.shape)],
        out_specs=rows(d),
        out_shape=jax.ShapeDtypeStruct((m, d), BF16),
        scratch_shapes=[pltpu.VMEM((tm, di), BF16)],
        compiler_params=_params(("arbitrary",)),
        name="mix",
    )(y, z_act, z_act, pa, ya, norm_g, w_ssm)


def _out_proj_kernel(mixed_ref, x_ref, wo_ref, h_ref):
    h_ref[...] = x_ref[...] + _dot(mixed_ref[...], wo_ref[...])


def _out_proj(mixed, x2, w_o, *, tm):
    m, d = x2.shape

    def rows():
        return pl.BlockSpec((tm, d), lambda i: (i, 0))

    return pl.pallas_call(
        _out_proj_kernel,
        grid=(m // tm,),
        in_specs=[rows(), rows(), _resident(w_o.shape)],
        out_specs=rows(),
        out_shape=jax.ShapeDtypeStruct((m, d), F32),
        compiler_params=_params(("arbitrary",)),
        name="out_proj",
    )(mixed, x2, w_o)


def _ffn_kernel(h_ref, g_ref, wg_ref, wu_ref, wd_ref, gf_ref, out_ref, u_ref, *, final_norm):
    f = pl.program_id(1)

    @pl.when(f == 0)
    def _():
        h = h_ref[...]
        u_ref[...] = _rmsnorm(h, g_ref[...]).astype(BF16)
        out_ref[...] = h

    u = u_ref[...]
    tf = wg_ref.shape[1]
    part = None
    for lo in range(0, tf, tf // 2):
        cols = slice(lo, lo + tf // 2)
        gate = _dot(u, wg_ref[:, cols])
        act = (gate * _sigmoid(gate) * _dot(u, wu_ref[:, cols])).astype(BF16)
        p = _dot(act, wd_ref[cols, :])
        part = p if part is None else part + p
    out_ref[...] += part

    if final_norm:
        @pl.when(f == pl.num_programs(1) - 1)
        def _():
            out_ref[...] = _rmsnorm(out_ref[...], gf_ref[...])


def _ffn(h, g_ffn, w_gate, w_up, w_down, g_final, *, tm, tf, final_norm):
    m, d = h.shape
    ff = w_gate.shape[1]
    return pl.pallas_call(
        functools.partial(_ffn_kernel, final_norm=final_norm),
        grid=(m // tm, ff // tf),
        in_specs=[pl.BlockSpec((tm, d), lambda i, f: (i, 0)),
                  pl.BlockSpec((1, d), lambda i, f: (0, 0)),
                  pl.BlockSpec((d, tf), lambda i, f: (0, f)),
                  pl.BlockSpec((d, tf), lambda i, f: (0, f)),
                  pl.BlockSpec((tf, d), lambda i, f: (f, 0)),
                  pl.BlockSpec((1, d), lambda i, f: (0, 0))],
        out_specs=pl.BlockSpec((tm, d), lambda i, f: (i, 0)),
        out_shape=jax.ShapeDtypeStruct((m, d), F32),
        scratch_shapes=[pltpu.VMEM((tm, d), BF16)],
        compiler_params=_params(("arbitrary", "arbitrary")),
        name="ffn",
    )(h, g_ffn, w_gate, w_up, w_down, g_final)


def _pick(n, candidates):
    for c in candidates:
        if n % c == 0:
            return c
    raise ValueError(f"no tile in {candidates} divides {n}")


def kernel(x, norm_mix_g, w_in, conv_a_w, w_out_a, ssm_conv_w, ssm_conv_b, dt_bias, a_log, d_skip,
           ssm_norm_g, w_out_ssm, w_o, norm_ffn_g, w_ffn_gate, w_ffn_up, w_ffn_down, norm_final_g):
    batch, seq, d = x.shape
    m = batch * seq
    depth = w_in.shape[0]
    d_inner = w_out_ssm.shape[1]
    heads = dt_bias.shape[1]
    gn = SSM_GROUPS * SSM_STATE
    conv_dim = d_inner + 2 * gn
    n_main = w_in.shape[2] - heads
    off_a = 2 * d
    off_z = off_a + 3 * conv_a_w.shape[2]
    off_xbc = off_z + d_inner
    hp = d_inner // heads
    tn = d
    assert conv_a_w.shape[2] == d and off_xbc + conv_dim == n_main
    assert d_inner == 2 * tn and 2 * gn == tn
    assert seq % SSM_CHUNK == 0 and heads <= LANES and heads % (2 * SSM_GROUPS) == 0
    assert 2 * hp == LANES and SSM_STATE == LANES and SSM_CHUNK == LANES
    pad_h = ((0, 0), (0, LANES - heads))
    tm_proj = _pick(seq, (1024, 512, 256, 128))

    h = x.reshape(m, d)
    for i in range(depth):
        w_all = w_in.astype(BF16).reshape(depth * d, w_in.shape[2])[i * d:(i + 1) * d]
        w_dt = jnp.pad(w_all[:, n_main:], pad_h)
        dtb = jnp.pad(dt_bias[i][None], pad_h)
        alog = jnp.pad(a_log[i][None], pad_h)
        d_skip_e = jnp.repeat(d_skip[i], hp)[None]

        pa, dt_raw, u = _norm_proj(h, norm_mix_g[i][None], w_all, w_dt, col0=0, n_tiles=off_z // tn,
                                   tm=tm_proj, tn=tn, epilogue="none")
        z_act = _proj(u, w_all, col0=off_z // tn, n_tiles=d_inner // tn,
                      tm=tm_proj, tn=tn, epilogue="silu")
        xbc = _proj_conv(u, w_all, ssm_conv_w[i], ssm_conv_b[i][None], col0=off_xbc // tn,
                         n_tiles=conv_dim // tn, seq=seq, tm=tm_proj, tn=tn)

        ya = _mixer_a(pa, conv_a_w[i], w_out_a[i].astype(BF16), seq=seq,
                      tm=_pick(seq, (512, 256, 128)), tile_gate=0, tile_b=off_a // tn)
        cs, rt, wt = _ssd_prep(dt_raw, dtb, alog, tr=tm_proj)
        y = _ssd(xbc, cs, rt, wt, d_skip_e, batch=batch, seq=seq,
                 chunks_per_step=_pick(seq // SSM_CHUNK, (SSD_CHUNKS_PER_STEP, 2, 1)))
        mixed = _mix(y, z_act, pa, ya, ssm_norm_g[i][None], w_out_ssm[i].astype(BF16),
                     tm=_pick(m, (512, 256, 128)), tile_gate_b=1)
        h = _out_proj(mixed, h, w_o[i].astype(BF16), tm=_pick(m, (1024, 512, 256, 128)))
        h = _ffn(h, norm_ffn_g[i][None], w_ffn_gate[i].astype(BF16), w_ffn_up[i].astype(BF16),
                 w_ffn_down[i].astype(BF16), norm_final_g[None],
                 tm=_pick(m, (1024, 512, 256, 128)), tf=_pick(w_ffn_gate.shape[2], (512, 256, 128)),
                 final_norm=(i == depth - 1))
    return h.reshape(batch, seq, d)
```
